```python
import math
import jax, jax.numpy as jnp
from jax import lax
import numpy as np

D_MODEL = 1024
BATCH = 2
SEQ = 8192
DEPTH = 2

HEAD_DIM = 64
Q_BLOCK = 128
NORM_EPS = 1e-6

RWKV_HEADS = 8
RWKV_WIDTH = RWKV_HEADS * HEAD_DIM
RWKV_DECAY_RANK = 64
RWKV_ICLR_RANK = 64
RWKV_GATE_RANK = 128
RWKV_GN_EPS = 64e-5

NSA_Q_HEADS = 8
NSA_KV_GROUPS = 2
NSA_CMP_LEN = 32
NSA_CMP_STRIDE = 16
NSA_CMP_HIDDEN = 256
NSA_SEL_BLOCK = 64
NSA_SEL_TOPN = 16
NSA_WINDOW = 512
NSA_FORCE_SCORE = 1e9

SWA_Q_HEADS = 8
SWA_KV_HEADS = 2
SWA_WINDOW = 128

REL_BUCKETS = 32
REL_MAX_DIST = 128

PEER_HEADS = 8
PEER_NKEYS = 128
PEER_EXPERTS = PEER_NKEYS * PEER_NKEYS
PEER_QDIM = 256
PEER_TOPK = 16
PEER_TOK_BLOCK = 128

RWKV_SPLITS = (RWKV_WIDTH, RWKV_WIDTH, RWKV_WIDTH, RWKV_DECAY_RANK, RWKV_ICLR_RANK, RWKV_GATE_RANK)
NSA_SPLITS = (NSA_Q_HEADS * HEAD_DIM,) + (NSA_KV_GROUPS * HEAD_DIM,) * 6 + (NSA_Q_HEADS * 3,)
SWA_SPLITS = (SWA_Q_HEADS * HEAD_DIM, SWA_KV_HEADS * HEAD_DIM, SWA_KV_HEADS * HEAD_DIM)
GATE_SPLITS = (D_MODEL, D_MODEL, D_MODEL)
RWKV_COLS = sum(RWKV_SPLITS)
IN_SPLITS = RWKV_SPLITS + NSA_SPLITS + SWA_SPLITS + GATE_SPLITS
IN_COLS = sum(IN_SPLITS)

kernel_name = 'hybrid_rwkv7_nsa_swasink_peer'


def split_cols(z, sizes):
    cuts = [int(c) for c in np.cumsum(sizes)[:-1]]
    return jnp.split(z, cuts, axis=-1)


def rmsnorm(x, w):
    xf = x.astype(jnp.float32)
    y = xf * lax.rsqrt(jnp.mean(xf * xf, axis=-1, keepdims=True) + NORM_EPS)
    return (y * w.astype(jnp.float32)).astype(x.dtype)


def token_shift(z, mu):
    z_prev = jnp.pad(z, ((0, 0), (1, 0), (0, 0)))[:, :-1]
    return z + mu * (z_prev - z)


def t5_bucket(dist):
    n = jnp.maximum(dist, 0)
    max_exact = REL_BUCKETS // 2
    nf = jnp.maximum(n, max_exact).astype(jnp.float32)
    large = max_exact + (jnp.log(nf / max_exact) / math.log(REL_MAX_DIST / max_exact)
                         * (REL_BUCKETS - max_exact)).astype(jnp.int32)
    large = jnp.minimum(large, REL_BUCKETS - 1)
    return jnp.where(n < max_exact, n, large)


def masked_softmax(s, mask):
    s = jnp.where(mask, s.astype(jnp.float32), -jnp.inf)
    m = jnp.max(s, axis=-1, keepdims=True)
    m = jnp.where(jnp.isfinite(m), m, 0.0)
    e = jnp.exp(s - m)
    return e / jnp.maximum(jnp.sum(e, axis=-1, keepdims=True), 1e-30)


def sink_softmax(s, mask, sink):
    s = jnp.where(mask, s.astype(jnp.float32), -jnp.inf)
    sink = sink.astype(jnp.float32)[None, :, :, None, None]
    m = jnp.maximum(jnp.max(s, axis=-1, keepdims=True), sink)
    e = jnp.exp(s - m)
    return e / (jnp.sum(e, axis=-1, keepdims=True) + jnp.exp(sink - m))


def rwkv7_mixer(zr, zk, zv, zw, za, zg, w0, w2, a0, a2, g2, k_k, k_a, r_k, ln_w, ln_b):
    B, S, C = zr.shape
    H, N = RWKV_HEADS, HEAD_DIM
    f32 = jnp.float32
    log_w = -jax.nn.softplus(-(w0 + jnp.tanh(zw) @ w2)) - 0.5
    decay = jnp.exp(-jnp.exp(log_w.astype(f32)))
    a = jax.nn.sigmoid(a0 + za @ a2)
    g = jax.nn.sigmoid(zg) @ g2
    kk = (zk * k_k).reshape(B, S, H, N).astype(f32)
    kk = kk * lax.rsqrt(jnp.maximum(jnp.sum(kk * kk, axis=-1, keepdims=True), 1e-12))
    k = zk * (1.0 + (a - 1.0) * k_a)
    hd = lambda z: z.reshape(B, S, H, N).astype(f32)
    r_h, k_h, v_h, a_h, w_h = hd(zr), hd(k), hd(zv), hd(a), hd(decay)
    tm = lambda z: jnp.swapaxes(z, 0, 1)

    def step(state, inp):
        r_t, w_t, k_t, v_t, kk_t, a_t = inp
        sa = jnp.einsum('bhij,bhj->bhi', state, -kk_t)
        state = (state * w_t[:, :, None, :] + sa[..., None] * (kk_t * a_t)[:, :, None, :]
                 + v_t[..., None] * k_t[:, :, None, :])
        return state, jnp.einsum('bhij,bhj->bhi', state, r_t)

    s0 = jnp.zeros((B, H, N, N), f32)
    _, y = lax.scan(step, s0, (tm(r_h), tm(w_h), tm(k_h), tm(v_h), tm(kk), tm(a_h)))
    y = tm(y)
    mu = jnp.mean(y, axis=-1, keepdims=True)
    var = jnp.mean(jnp.square(y - mu), axis=-1, keepdims=True)
    y = ((y - mu) * lax.rsqrt(var + RWKV_GN_EPS)).reshape(B, S, C) * ln_w + ln_b
    bonus = jnp.sum(r_h * k_h * r_k, axis=-1, keepdims=True) * v_h
    out = (y + bonus.reshape(B, S, C)) * g
    return out.astype(zr.dtype)


def nsa_compress(z, pe, w1, w2):
    B, S, G, dh = z.shape
    n_sub = NSA_CMP_LEN // NSA_CMP_STRIDE
    chunks = z.reshape(B, S // NSA_CMP_STRIDE, NSA_CMP_STRIDE, G, dh)
    n_cmp = S // NSA_CMP_STRIDE - n_sub + 1
    blocks = jnp.concatenate([chunks[:, j:j + n_cmp] for j in range(n_sub)], axis=2)
    blocks = blocks + pe[None, None, :, None, :]
    flat = blocks.transpose(0, 3, 1, 2, 4).reshape(B, G, n_cmp, NSA_CMP_LEN * dh)
    return jax.nn.gelu(flat @ w1, approximate=False) @ w2


def nsa_mixer(q, kc, vc, ks, vs, kw, vw, gates, pe_k, pe_v, ck_w1, ck_w2, cv_w1, cv_w2, bias_tbl):
    B, S, _ = q.shape
    G, Hg, dh = NSA_KV_GROUPS, NSA_Q_HEADS // NSA_KV_GROUPS, HEAD_DIM
    scale = dh ** -0.5
    heads = lambda z: z.reshape(B, S, G, dh)
    kv_t = lambda z: heads(z).transpose(0, 2, 1, 3)
    qh = q.reshape(B, S, G, Hg, dh).transpose(0, 2, 3, 1, 4)
    gh = jax.nn.sigmoid(gates.reshape(B, S, G, Hg, 3)).transpose(0, 2, 3, 1, 4)

    k_cmp = nsa_compress(heads(kc), pe_k, ck_w1, ck_w2)
    v_cmp = nsa_compress(heads(vc), pe_v, cv_w1, cv_w2)
    n_cmp = k_cmp.shape[2]
    cmp_start = jnp.arange(n_cmp) * NSA_CMP_STRIDE
    cmp_end = cmp_start + NSA_CMP_LEN - 1
    n_sel = S // NSA_SEL_BLOCK
    sel_start = jnp.arange(n_sel) * NSA_SEL_BLOCK
    overlap = ((cmp_end[:, None] >= sel_start[None, :])
               & (cmp_start[:, None] <= sel_start[None, :] + NSA_SEL_BLOCK - 1)).astype(jnp.float32)
    ks_blk = kv_t(ks).reshape(B, G, n_sel, NSA_SEL_BLOCK, dh)
    vs_blk = kv_t(vs).reshape(B, G, n_sel, NSA_SEL_BLOCK, dh)
    pad = ((0, 0), (0, 0), (NSA_WINDOW, 0), (0, 0))
    kw_pad = jnp.pad(kv_t(kw), pad)
    vw_pad = jnp.pad(kv_t(vw), pad)

    tbl = bias_tbl.T.reshape(G, Hg, REL_BUCKETS)
    win_len = NSA_WINDOW + Q_BLOCK
    rel = jnp.arange(Q_BLOCK)[:, None] + NSA_WINDOW - jnp.arange(win_len)[None, :]
    win_bias = tbl[:, :, t5_bucket(rel)]
    win_rel_mask = (rel >= 0) & (rel < NSA_WINDOW)
    top_n = min(NSA_SEL_TOPN, n_sel)
    b_ix = jnp.arange(B)[:, None, None, None]
    g_ix = jnp.arange(G)[None, :, None, None]
    g_ix5 = jnp.arange(G)[None, :, None, None, None]
    h_ix5 = jnp.arange(Hg)[None, None, :, None, None]
    blk = jnp.arange(n_sel)

    def block(i):
        s0 = i * Q_BLOCK
        t = s0 + jnp.arange(Q_BLOCK)
        qb = lax.dynamic_slice_in_dim(qh, s0, Q_BLOCK, axis=3)
        gb = lax.dynamic_slice_in_dim(gh, s0, Q_BLOCK, axis=3)
        sc = jnp.einsum('bghqd,bgcd->bghqc', qb, k_cmp) * scale
        pc = masked_softmax(sc, cmp_end[None, :] <= t[:, None])
        o_cmp = jnp.einsum('bghqc,bgcd->bghqd', pc.astype(v_cmp.dtype), v_cmp)
        imp = jnp.einsum('bghqc,cn->bgqn', pc, overlap)
        cur = t // NSA_SEL_BLOCK
        forced = (blk[None, :] == 0) | (blk[None, :] == cur[:, None]) | (blk[None, :] == cur[:, None] - 1)
        valid = blk[None, :] <= cur[:, None]
        imp = jnp.where(forced, NSA_FORCE_SCORE, jnp.where(valid, imp, -1.0))
        _, sel = lax.top_k(imp, top_n)
        k_sel = ks_blk[b_ix, g_ix, sel].reshape(B, G, Q_BLOCK, top_n * NSA_SEL_BLOCK, dh)
        v_sel = vs_blk[b_ix, g_ix, sel].reshape(B, G, Q_BLOCK, top_n * NSA_SEL_BLOCK, dh)
        pos = (sel[..., None] * NSA_SEL_BLOCK + jnp.arange(NSA_SEL_BLOCK)).reshape(B, G, Q_BLOCK, top_n * NSA_SEL_BLOCK)
        dist = t[:, None] - pos
        bias = tbl[g_ix5, h_ix5, t5_bucket(dist)[:, :, None]]
        ss = jnp.einsum('bghqd,bgqkd->bghqk', qb, k_sel) * scale + bias
        ps = masked_softmax(ss, (dist >= 0)[:, :, None])
        o_sel = jnp.einsum('bghqk,bgqkd->bghqd', ps.astype(v_sel.dtype), v_sel)
        k_win = lax.dynamic_slice_in_dim(kw_pad, s0, win_len, axis=2)
        v_win = lax.dynamic_slice_in_dim(vw_pad, s0, win_len, axis=2)
        kpos = s0 - NSA_WINDOW + jnp.arange(win_len)
        sw = jnp.einsum('bghqd,bgkd->bghqk', qb, k_win) * scale + win_bias
        pw = masked_softmax(sw, win_rel_mask & (kpos >= 0)[None, :])
        o_win = jnp.einsum('bghqk,bgkd->bghqd', pw.astype(v_win.dtype), v_win)
        return gb[..., 0:1] * o_cmp + gb[..., 1:2] * o_sel + gb[..., 2:3] * o_win

    out = lax.map(block, jnp.arange(S // Q_BLOCK))
    return out.transpose(1, 0, 4, 2, 3, 5).reshape(B, S, G * Hg * dh)


def swa_sink_mixer(q, k, v, sinks, bias_tbl):
    B, S, _ = q.shape
    G, Hg, dh = SWA_KV_HEADS, SWA_Q_HEADS // SWA_KV_HEADS, HEAD_DIM
    scale = dh ** -0.5
    qh = q.reshape(B, S, G, Hg, dh).transpose(0, 2, 3, 1, 4)
    pad = ((0, 0), (0, 0), (SWA_WINDOW, 0), (0, 0))
    k_pad = jnp.pad(k.reshape(B, S, G, dh).transpose(0, 2, 1, 3), pad)
    v_pad = jnp.pad(v.reshape(B, S, G, dh).transpose(0, 2, 1, 3), pad)
    win_len = SWA_WINDOW + Q_BLOCK
    rel = jnp.arange(Q_BLOCK)[:, None] + SWA_WINDOW - jnp.arange(win_len)[None, :]
    bias = bias_tbl.T.reshape(G, Hg, REL_BUCKETS)[:, :, t5_bucket(rel)]
    rel_mask = (rel >= 0) & (rel < SWA_WINDOW)
    sink = sinks.reshape(G, Hg)

    def block(i):
        s0 = i * Q_BLOCK
        qb = lax.dynamic_slice_in_dim(qh, s0, Q_BLOCK, axis=3)
        kb = lax.dynamic_slice_in_dim(k_pad, s0, win_len, axis=2)
        vb = lax.dynamic_slice_in_dim(v_pad, s0, win_len, axis=2)
        kpos = s0 - SWA_WINDOW + jnp.arange(win_len)
        s = jnp.einsum('bghqd,bgkd->bghqk', qb, kb) * scale + bias
        p = sink_softmax(s, rel_mask & (kpos >= 0)[None, :], sink)
        return jnp.einsum('bghqk,bgkd->bghqd', p.astype(vb.dtype), vb)

    out = lax.map(block, jnp.arange(S // Q_BLOCK))
    return out.transpose(1, 0, 4, 2, 3, 5).reshape(B, S, G * Hg * dh)


def peer_ffn(z, wq, subkeys, u_tab, v_tab):
    B, S, D = z.shape
    Hp, K, half = PEER_HEADS, PEER_TOPK, PEER_QDIM // 2
    zt = z.reshape(B * S // PEER_TOK_BLOCK, PEER_TOK_BLOCK, D)

    def block(zb):
        q = (zb @ wq).reshape(PEER_TOK_BLOCK, Hp, 2, half)
        s = jnp.einsum('thpd,hpnd->thpn', q, subkeys).astype(jnp.float32)
        s_top, i_top = lax.top_k(s, K)
        cand = (s_top[:, :, 0, :, None] + s_top[:, :, 1, None, :]).reshape(PEER_TOK_BLOCK, Hp, K * K)
        score, flat = lax.top_k(cand, K)
        i1 = jnp.take_along_axis(i_top[:, :, 0], flat // K, axis=-1)
        i2 = jnp.take_along_axis(i_top[:, :, 1], flat % K, axis=-1)
        expert = i1 * PEER_NKEYS + i2
        gate = jax.nn.softmax(score, axis=-1)
        h = jnp.einsum('thkd,td->thk', u_tab[expert], zb).astype(jnp.float32)
        act = (gate * jax.nn.gelu(h, approximate=False)).astype(zb.dtype)
        return jnp.einsum('thk,thkd->td', act, v_tab[expert])

    return lax.map(block, zt).reshape(B, S, D)


def setup_inputs(seed: int = 0) -> dict:
    key = jax.random.key(seed)
    keys = iter(jax.random.split(key, 40))
    L, D = DEPTH, D_MODEL

    def nrm(shape, scale):
        return jax.random.normal(next(keys), shape, jnp.float32) * scale

    def unif(shape, lo, hi):
        return jax.random.uniform(next(keys), shape, jnp.float32, lo, hi)

    cmp_in = NSA_CMP_LEN * HEAD_DIM
    return {
        'x': nrm((BATCH, SEQ, D), 1.0),
        'ln1_w': 1.0 + nrm((L, D), 0.02),
        'ln2_w': 1.0 + nrm((L, D), 0.02),
        'lnf_w': 1.0 + nrm((D,), 0.02),
        'rel_bias': nrm((REL_BUCKETS, NSA_Q_HEADS + SWA_Q_HEADS), 0.3),
        'w_in': nrm((L, D, IN_COLS), D ** -0.5),
        'rwkv_mu': unif((L, RWKV_COLS), 0.0, 1.0),
        'rwkv_w0': -0.5 + nrm((L, RWKV_WIDTH), 0.5),
        'rwkv_w2': nrm((L, RWKV_DECAY_RANK, RWKV_WIDTH), 0.1),
        'rwkv_a0': nrm((L, RWKV_WIDTH), 0.3),
        'rwkv_a2': nrm((L, RWKV_ICLR_RANK, RWKV_WIDTH), 0.1),
        'rwkv_g2': nrm((L, RWKV_GATE_RANK, RWKV_WIDTH), RWKV_GATE_RANK ** -0.5),
        'rwkv_k_k': 0.85 + nrm((L, RWKV_WIDTH), 0.05),
        'rwkv_k_a': 1.0 + nrm((L, RWKV_WIDTH), 0.05),
        'rwkv_r_k': nrm((L, RWKV_HEADS, HEAD_DIM), 0.1),
        'rwkv_ln_w': 1.0 + nrm((L, RWKV_WIDTH), 0.02),
        'rwkv_ln_b': nrm((L, RWKV_WIDTH), 0.02),
        'nsa_pe_k': nrm((L, NSA_CMP_LEN, HEAD_DIM), 0.1),
        'nsa_pe_v': nrm((L, NSA_CMP_LEN, HEAD_DIM), 0.1),
        'nsa_ck_w1': nrm((L, cmp_in, NSA_CMP_HIDDEN), cmp_in ** -0.5),
        'nsa_ck_w2': nrm((L, NSA_CMP_HIDDEN, HEAD_DIM), NSA_CMP_HIDDEN ** -0.5),
        'nsa_cv_w1': nrm((L, cmp_in, NSA_CMP_HIDDEN), cmp_in ** -0.5),
        'nsa_cv_w2': nrm((L, NSA_CMP_HIDDEN, HEAD_DIM), NSA_CMP_HIDDEN ** -0.5),
        'swa_sinks': nrm((L, SWA_Q_HEADS), 0.5),
        'w_br_a': nrm((L, RWKV_WIDTH, D), RWKV_WIDTH ** -0.5),
        'w_br_b': nrm((L, NSA_Q_HEADS * HEAD_DIM, D), (NSA_Q_HEADS * HEAD_DIM) ** -0.5),
        'w_br_c': nrm((L, SWA_Q_HEADS * HEAD_DIM, D), (SWA_Q_HEADS * HEAD_DIM) ** -0.5),
        'w_out': nrm((L, D, D), D ** -0.5),
        'peer_wq': nrm((L, D, PEER_HEADS * PEER_QDIM), D ** -0.5),
        'peer_subkeys': nrm((L, PEER_HEADS, 2, PEER_NKEYS, PEER_QDIM // 2), (PEER_QDIM // 2) ** -0.5),
        'peer_u': nrm((L, PEER_EXPERTS, D), D ** -0.5),
        'peer_v': nrm((L, PEER_EXPERTS, D), 0.5 * PEER_HEADS ** -0.5),
    }


def reference(x, ln1_w, ln2_w, lnf_w, rel_bias, w_in, rwkv_mu, rwkv_w0, rwkv_w2, rwkv_a0, rwkv_a2,
              rwkv_g2, rwkv_k_k, rwkv_k_a, rwkv_r_k, rwkv_ln_w, rwkv_ln_b, nsa_pe_k, nsa_pe_v,
              nsa_ck_w1, nsa_ck_w2, nsa_cv_w1, nsa_cv_w2, swa_sinks, w_br_a, w_br_b, w_br_c, w_out,
              peer_wq, peer_subkeys, peer_u, peer_v):
    nsa_tbl = rel_bias[:, :NSA_Q_HEADS]
    swa_tbl = rel_bias[:, NSA_Q_HEADS:]
    for l in range(DEPTH):
        u = rmsnorm(x, ln1_w[l])
        proj = u @ w_in[l]
        rw = token_shift(proj[..., :RWKV_COLS], rwkv_mu[l])
        zr, zk, zv, zw, za, zg = split_cols(rw, RWKV_SPLITS)
        cols = split_cols(proj[..., RWKV_COLS:], NSA_SPLITS + SWA_SPLITS + GATE_SPLITS)
        nq, nkc, nvc, nks, nvs, nkw, nvw, ngate = cols[:8]
        sq, sk, sv = cols[8:11]
        g_a, g_b, g_c = [jax.nn.sigmoid(c) for c in cols[11:14]]
        o_a = rwkv7_mixer(zr, zk, zv, zw, za, zg, rwkv_w0[l], rwkv_w2[l], rwkv_a0[l], rwkv_a2[l],
                          rwkv_g2[l], rwkv_k_k[l], rwkv_k_a[l], rwkv_r_k[l], rwkv_ln_w[l], rwkv_ln_b[l])
        o_b = nsa_mixer(nq, nkc, nvc, nks, nvs, nkw, nvw, ngate, nsa_pe_k[l], nsa_pe_v[l],
                        nsa_ck_w1[l], nsa_ck_w2[l], nsa_cv_w1[l], nsa_cv_w2[l], nsa_tbl)
        o_c = swa_sink_mixer(sq, sk, sv, swa_sinks[l], swa_tbl)
        merged = g_a * (o_a @ w_br_a[l]) + g_b * (o_b @ w_br_b[l]) + g_c * (o_c @ w_br_c[l])
        x = x + merged @ w_out[l]
        x = x + peer_ffn(rmsnorm(x, ln2_w[l]), peer_wq[l], peer_subkeys[l], peer_u[l], peer_v[l])
    return rmsnorm(x, lnf_w)
```

```python
import functools
import math

import jax
import jax.numpy as jnp
from jax import lax
from jax.experimental import pallas as pl
from jax.experimental.pallas import tpu as pltpu

F32 = jnp.float32
BF16 = jnp.bfloat16

D_MODEL = 1024
HEAD_DIM = 64
Q_BLOCK = 128
NORM_EPS = 1e-6

RWKV_HEADS = 8
RWKV_WIDTH = RWKV_HEADS * HEAD_DIM
RWKV_DECAY_RANK = 64
RWKV_ICLR_RANK = 64
RWKV_GATE_RANK = 128
RWKV_GN_EPS = 64e-5
RWKV_COLS = 3 * RWKV_WIDTH + RWKV_DECAY_RANK + RWKV_ICLR_RANK + RWKV_GATE_RANK
RWKV_CHUNK = 64

NSA_Q_HEADS = 8
NSA_KV_GROUPS = 2
NSA_HG = NSA_Q_HEADS // NSA_KV_GROUPS
NSA_CMP_LEN = 32
NSA_CMP_STRIDE = 16
NSA_CMP_HIDDEN = 256
NSA_SEL_BLOCK = 64
NSA_SEL_TOPN = 16
NSA_WINDOW = 512
NSA_FORCE_SCORE = 1e9

SWA_Q_HEADS = 8
SWA_KV_HEADS = 2
SWA_HG = SWA_Q_HEADS // SWA_KV_HEADS
SWA_WINDOW = 128

REL_BUCKETS = 32
REL_MAX_DIST = 128

PEER_HEADS = 8
PEER_NKEYS = 128
PEER_QDIM = 256
PEER_TOPK = 16

NEG_BIG = -1e30
VMEM_LIMIT_BYTES = 56 * 1024 * 1024


def _params(*sem):
    return pltpu.CompilerParams(dimension_semantics=sem, vmem_limit_bytes=VMEM_LIMIT_BYTES)


def _dot(a, b):
    return jnp.dot(a.astype(BF16), b.astype(BF16), preferred_element_type=F32)


def _dot_nt(a, b):
    return lax.dot_general(a.astype(BF16), b.astype(BF16), (((1,), (1,)), ((), ())),
                           preferred_element_type=F32)


def _dot_tn(a, b):
    return lax.dot_general(a.astype(BF16), b.astype(BF16), (((0,), (0,)), ((), ())),
                           preferred_element_type=F32)


def _split_dot(a, b_exact):
    hi = a.astype(BF16)
    lo = (a - hi.astype(F32)).astype(BF16)
    return (jnp.dot(hi, b_exact, preferred_element_type=F32)
            + jnp.dot(lo, b_exact, preferred_element_type=F32))


def _split_dot_rhs(a_exact, b):
    hi = b.astype(BF16)
    lo = (b - hi.astype(F32)).astype(BF16)
    return (jnp.dot(a_exact, hi, preferred_element_type=F32)
            + jnp.dot(a_exact, lo, preferred_element_type=F32))


def _sigmoid(x):
    return 1.0 / (1.0 + jnp.exp(-x))


def _gelu_exact(x):
    return 0.5 * x * (1.0 + lax.erf(x * (1.0 / math.sqrt(2.0))))


def _rms(x, g):
    return x * lax.rsqrt(jnp.mean(x * x, axis=-1, keepdims=True) + NORM_EPS) * g


def _rms_matmul_kernel(x_ref, g_ref, w_ref, o_ref):
    y = _rms(x_ref[...], g_ref[...])
    o_ref[...] = jnp.dot(y.astype(BF16), w_ref[...], preferred_element_type=F32)


def _rms_matmul(x, g, w, tm, tn):
    t, d = x.shape
    n = w.shape[1]
    return pl.pallas_call(
        _rms_matmul_kernel,
        grid=(t // tm, n // tn),
        in_specs=[pl.BlockSpec((tm, d), lambda i, j: (i, 0)),
                  pl.BlockSpec((1, d), lambda i, j: (0, 0)),
                  pl.BlockSpec((d, tn), lambda i, j: (0, j))],
        out_specs=pl.BlockSpec((tm, tn), lambda i, j: (i, j)),
        out_shape=jax.ShapeDtypeStruct((t, n), F32),
        compiler_params=_params("parallel", "arbitrary"),
        name="rms_proj",
    )(x, g.reshape(1, d), w)


def _rwkv_kernel(z_ref, w0_ref, a0_ref, kk_ref, ka_ref, rk_ref, lnw_ref, lnb_ref,
                 w2_ref, a2_ref, g2_ref, tri_ref, bd_ref, o_ref, st_ref):
    c = RWKV_CHUNK
    n = HEAD_DIM
    w_ = RWKV_WIDTH

    @pl.when(pl.program_id(1) == 0)
    def _():
        st_ref[...] = jnp.zeros_like(st_ref)

    z = z_ref[0]
    zr = z[:, 0:w_]
    zk = z[:, w_:2 * w_]
    zv = z[:, 2 * w_:3 * w_]
    o1 = 3 * w_
    zw = z[:, o1:o1 + RWKV_DECAY_RANK]
    za = z[:, o1 + RWKV_DECAY_RANK:o1 + RWKV_DECAY_RANK + RWKV_ICLR_RANK]
    zg = z[:, o1 + RWKV_DECAY_RANK + RWKV_ICLR_RANK:]
    bd = bd_ref[...]
    tri = tri_ref[...]

    y = -(w0_ref[...] + _dot(jnp.tanh(zw), w2_ref[...]))
    softplus = jnp.maximum(y, 0.0) + jnp.log(1.0 + jnp.exp(-jnp.abs(y)))
    lw = -jnp.exp(-softplus - 0.5)
    a = _sigmoid(a0_ref[...] + _dot(za, a2_ref[...]))
    g = _dot(_sigmoid(zg), g2_ref[...])
    kk = zk * kk_ref[...]
    kk = kk * lax.rsqrt(jnp.maximum(_split_dot(kk * kk, bd), 1e-12))
    k = zk * (1.0 + (a - 1.0) * ka_ref[...])

    cum = _split_dot_rhs(tri, lw)
    e_pos = jnp.exp(cum)
    e_neg = jnp.exp(-cum)
    a_t = -kk * jnp.exp(cum - lw)
    b_t = kk * a * e_neg
    k_t = k * e_neg
    r_t = zr * e_pos

    ri = lax.broadcasted_iota(jnp.int32, (c, c), 0)
    ci = lax.broadcasted_iota(jnp.int32, (c, c), 1)
    strict = ri > ci
    incl = ri >= ci
    eye = (ri == ci).astype(F32)

    ys = []
    for h in range(RWKV_HEADS):
        sl = slice(h * n, (h + 1) * n)
        ah, bh, kh, rh, vh = a_t[:, sl], b_t[:, sl], k_t[:, sl], r_t[:, sl], zv[:, sl]
        s0 = st_ref[h]
        l_ab = jnp.where(strict, _dot_nt(ah, bh), 0.0)
        l_ak = jnp.where(strict, _dot_nt(ah, kh), 0.0)
        m_rb = jnp.where(incl, _dot_nt(rh, bh), 0.0)
        m_rk = jnp.where(incl, _dot_nt(rh, kh), 0.0)
        p = l_ab
        tinv = eye + p
        for _ in range(int(math.log2(c)) - 1):
            p = _dot(p, p)
            tinv = tinv + _dot(p, tinv)
        u = _dot(tinv, _dot_nt(ah, s0) + _dot(l_ak, vh))
        ys.append(_dot_nt(rh, s0) + _dot(m_rb, u) + _dot(m_rk, vh))
        w_c = e_pos[c - 1:c, sl]
        st_ref[h] = (s0 + _dot_tn(u, bh) + _dot_tn(vh, kh)) * w_c
    yv = jnp.concatenate(ys, axis=-1)

    mu = _split_dot(yv, bd) * (1.0 / n)
    dlt = yv - mu
    var = _split_dot(dlt * dlt, bd) * (1.0 / n)
    yn = dlt * lax.rsqrt(var + RWKV_GN_EPS) * lnw_ref[...] + lnb_ref[...]
    bonus = _split_dot(zr * k * rk_ref[...], bd) * zv
    o_ref[0] = (yn + bonus) * g


def _rwkv(rw, w0, w2, a0, a2, g2, k_k, k_a, r_k, ln_w, ln_b):
    b, s, cols = rw.shape
    c = RWKV_CHUNK
    w_ = RWKV_WIDTH
    row = lambda v: v.reshape(1, w_).astype(F32)
    idx = jnp.arange(w_) // HEAD_DIM
    bd = (idx[:, None] == idx[None, :]).astype(BF16)
    tri = (jnp.arange(c)[:, None] >= jnp.arange(c)[None, :]).astype(BF16)
    full = lambda shp: pl.BlockSpec(shp, lambda i, j: (0,) * len(shp))
    return pl.pallas_call(
        _rwkv_kernel,
        grid=(b, s // c),
        in_specs=[pl.BlockSpec((1, c, cols), lambda i, j: (i, j, 0))]
                 + [full((1, w_))] * 7
                 + [full((RWKV_DECAY_RANK, w_)), full((RWKV_ICLR_RANK, w_)), full((RWKV_GATE_RANK, w_)),
                    full((c, c)), full((w_, w_))],
        out_specs=pl.BlockSpec((1, c, w_), lambda i, j: (i, j, 0)),
        out_shape=jax.ShapeDtypeStruct((b, s, w_), F32),
        scratch_shapes=[pltpu.VMEM((RWKV_HEADS, HEAD_DIM, HEAD_DIM), F32)],
        compiler_params=_params("parallel", "arbitrary"),
        name="rwkv7_chunked",
    )(rw, row(w0), row(a0), row(k_k), row(k_a), row(r_k), row(ln_w), row(ln_b),
      w2.astype(BF16), a2.astype(BF16), g2.astype(BF16), tri, bd)


def _cmp_kernel(f_ref, pe_ref, w1_ref, w2_ref, o_ref):
    f = f_ref[0, 0] + pe_ref[0]
    h = _gelu_exact(_dot(f, w1_ref[0]))
    o_ref[0, 0] = _dot(h, w2_ref[0])


def _nsa_compress(flat, pe, w1, w2):
    two, bg, ncp, lin = flat.shape
    hid = w1.shape[-1]
    return pl.pallas_call(
        _cmp_kernel,
        grid=(two, bg),
        in_specs=[pl.BlockSpec((1, 1, ncp, lin), lambda i, j: (i, j, 0, 0)),
                  pl.BlockSpec((1, 1, lin), lambda i, j: (i, 0, 0)),
                  pl.BlockSpec((1, lin, hid), lambda i, j: (i, 0, 0)),
                  pl.BlockSpec((1, hid, HEAD_DIM), lambda i, j: (i, 0, 0))],
        out_specs=pl.BlockSpec((1, 1, ncp, HEAD_DIM), lambda i, j: (i, j, 0, 0)),
        out_shape=jax.ShapeDtypeStruct((two, bg, ncp, HEAD_DIM), F32),
        compiler_params=_params("parallel", "parallel"),
        name="nsa_compress",
    )(flat, pe, w1.astype(BF16), w2.astype(BF16))


def _attn_update(h, q, k_t, v, bias, mask, m_ref, l_ref, acc_ref):
    s = jnp.dot(q, k_t, preferred_element_type=F32) + bias
    if mask is not None:
        s = jnp.where(mask, s, NEG_BIG)
    m_old = m_ref[h]
    m_new = jnp.maximum(m_old, jnp.max(s, axis=-1, keepdims=True))
    p = jnp.exp(s - m_new)
    if mask is not None:
        p = jnp.where(mask, p, 0.0)
    alpha = jnp.exp(m_old - m_new)
    l_ref[h] = alpha * l_ref[h] + jnp.sum(p, axis=-1, keepdims=True)
    acc_ref[h] = alpha * acc_ref[h] + jnp.dot(p.astype(BF16), v, preferred_element_type=F32)
    m_ref[h] = m_new


def _nsa_kernel(q_ref, gate_ref, kct_ref, vc_ref, kst_ref, vs_ref, kwt_ref, vw_ref, ovl_ref,
                b0_ref, b1_ref, c31_ref, o_ref, m_ref, l_ref, acc_ref, *, n_sel, n_cmp_pad):
    qb = Q_BLOCK
    i = pl.program_id(2)
    s0 = i * qb
    row = lax.broadcasted_iota(jnp.int32, (qb, qb), 0)
    col = lax.broadcasted_iota(jnp.int32, (qb, qb), 1)
    qs = [q_ref[0, 0, h] for h in range(NSA_HG)]

    ccol = lax.broadcasted_iota(jnp.int32, (qb, n_cmp_pad), 1)
    trow = s0 + lax.broadcasted_iota(jnp.int32, (qb, n_cmp_pad), 0)
    cmask = (NSA_CMP_STRIDE * ccol + NSA_CMP_LEN - 1) <= trow
    kct = kct_ref[0, 0]
    vc = vc_ref[0, 0]
    psum = jnp.zeros((qb, n_cmp_pad), F32)
    o_cmp = []
    for h in range(NSA_HG):
        s = jnp.dot(qs[h], kct, preferred_element_type=F32)
        s = jnp.where(cmask, s, NEG_BIG)
        mx = jnp.max(s, axis=-1, keepdims=True)
        e = jnp.where(cmask, jnp.exp(s - mx), 0.0)
        p = e / jnp.maximum(jnp.sum(e, axis=-1, keepdims=True), 1e-30)
        o_cmp.append(jnp.dot(p.astype(BF16), vc, preferred_element_type=F32))
        psum = psum + p

    imp = _split_dot(psum, ovl_ref[...])
    ncol = lax.broadcasted_iota(jnp.int32, (qb, n_sel), 1)
    cur = lax.shift_right_logical(s0 + lax.broadcasted_iota(jnp.int32, (qb, n_sel), 0),
                                  int(math.log2(NSA_SEL_BLOCK)))
    forced = (ncol == 0) | (ncol == cur) | (ncol == cur - 1)
    score = jnp.where(forced, NSA_FORCE_SCORE, jnp.where(ncol <= cur, imp, -1.0))
    sel = jnp.zeros((qb, n_sel), F32)
    for _ in range(min(NSA_SEL_TOPN, n_sel)):
        mx = jnp.max(score, axis=-1, keepdims=True)
        first = jnp.min(jnp.where(score == mx, ncol, n_sel), axis=-1, keepdims=True)
        hit = ncol == first
        sel = jnp.where(hit, 1.0, sel)
        score = jnp.where(hit, -jnp.inf, score)
    selb = sel.astype(BF16)

    def reset():
        m_ref[...] = jnp.full_like(m_ref, NEG_BIG)
        l_ref[...] = jnp.zeros_like(l_ref)
        acc_ref[...] = jnp.zeros_like(acc_ref)

    def finish():
        return [acc_ref[h] / jnp.maximum(l_ref[h], 1e-30) for h in range(NSA_HG)]

    erow = lax.broadcasted_iota(jnp.int32, (n_sel, qb), 0)
    ecol = lax.shift_right_logical(lax.broadcasted_iota(jnp.int32, (n_sel, qb), 1),
                                   int(math.log2(NSA_SEL_BLOCK)))
    blocks_per_tile = qb // NSA_SEL_BLOCK

    def sel_tile(j, bias_of, extra):
        expand = (erow == blocks_per_tile * j + ecol).astype(BF16)
        mask = jnp.dot(selb, expand, preferred_element_type=F32) > 0.5
        if extra is not None:
            mask = mask & extra
        k_t = kst_ref[0, 0, j]
        v = vs_ref[0, 0, j]
        for h in range(NSA_HG):
            _attn_update(h, qs[h], k_t, v, bias_of(h), mask, m_ref, l_ref, acc_ref)

    far_bias = lambda h: c31_ref[0, h, 0:1, :]
    reset()

    def far_body(j, carry):
        sel_tile(j, far_bias, None)
        return carry

    lax.fori_loop(0, jnp.maximum(i - 1, 0), far_body, 0)

    @pl.when(i >= 1)
    def _():
        sel_tile(i - 1, lambda h: b1_ref[0, h], None)

    sel_tile(i, lambda h: b0_ref[0, h], col <= row)
    o_sel = finish()

    reset()
    n_back = NSA_WINDOW // qb
    for d in range(n_back, -1, -1):
        if d == 0:
            mask, bias_of = col <= row, (lambda h: b0_ref[0, h])
        elif d == 1:
            mask, bias_of = None, (lambda h: b1_ref[0, h])
        elif d == n_back:
            mask, bias_of = row < col, far_bias
        else:
            mask, bias_of = None, far_bias
        def win_tile(d=d, mask=mask, bias_of=bias_of):
            j = i - d
            k_t = kwt_ref[0, 0, j]
            v = vw_ref[0, 0, j]
            for h in range(NSA_HG):
                _attn_update(h, qs[h], k_t, v, bias_of(h), mask, m_ref, l_ref, acc_ref)

        if d == 0:
            win_tile()
        else:
            pl.when(i >= d)(win_tile)
    o_win = finish()

    gs = _sigmoid(gate_ref[0, 0])
    outs = []
    for h in range(NSA_HG):
        g0 = gs[:, 3 * h + 0:3 * h + 1]
        g1 = gs[:, 3 * h + 1:3 * h + 2]
        g2 = gs[:, 3 * h + 2:3 * h + 3]
        outs.append(g0 * o_cmp[h] + g1 * o_sel[h] + g2 * o_win[h])
    o_ref[0] = jnp.concatenate(outs, axis=-1)


def _t5_bucket(dist):
    n = jnp.maximum(dist, 0)
    max_exact = REL_BUCKETS // 2
    nf = jnp.maximum(n, max_exact).astype(F32)
    large = max_exact + (jnp.log(nf / max_exact) / math.log(REL_MAX_DIST / max_exact)
                         * (REL_BUCKETS - max_exact)).astype(jnp.int32)
    large = jnp.minimum(large, REL_BUCKETS - 1)
    return jnp.where(n < max_exact, n, large)


def _bias_tiles(tbl, groups, hg):
    qi = jnp.arange(Q_BLOCK)[:, None]
    kj = jnp.arange(Q_BLOCK)[None, :]
    t = tbl.T.reshape(groups, hg, REL_BUCKETS)
    b0 = t[:, :, _t5_bucket(qi - kj)]
    b1 = t[:, :, _t5_bucket(qi - kj + Q_BLOCK)]
    far = jnp.broadcast_to(t[:, :, REL_BUCKETS - 1][:, :, None, None], (groups, hg, 8, Q_BLOCK))
    return b0.astype(F32), b1.astype(F32), far.astype(F32)


def _tiles_kt(z, b, s, g):
    nt = s // Q_BLOCK
    z = z.reshape(b, nt, Q_BLOCK, g, HEAD_DIM)
    return jnp.transpose(z, (0, 3, 1, 4, 2)).astype(BF16)


def _tiles_v(z, b, s, g):
    nt = s // Q_BLOCK
    z = z.reshape(b, nt, Q_BLOCK, g, HEAD_DIM)
    return jnp.transpose(z, (0, 3, 1, 2, 4)).astype(BF16)


def _heads_q(q, b, s, g, hg):
    q = q.reshape(b, s, g, hg, HEAD_DIM) * (HEAD_DIM ** -0.5)
    return jnp.transpose(q, (0, 2, 3, 1, 4)).astype(BF16)


def _nsa(q, kc, vc, ks, vs, kw, vw, gates, pe_k, pe_v, ck_w1, ck_w2, cv_w1, cv_w2, tbl):
    b, s, _ = q.shape
    g, hg, dh = NSA_KV_GROUPS, NSA_HG, HEAD_DIM
    nchunk = s // NSA_CMP_STRIDE
    n_sub = NSA_CMP_LEN // NSA_CMP_STRIDE
    n_cmp = nchunk - n_sub + 1
    n_sel = s // NSA_SEL_BLOCK
    nt = s // Q_BLOCK

    def flat_blocks(z):
        ch = z.reshape(b, nchunk, NSA_CMP_STRIDE, g, dh)
        ch = jnp.pad(ch, ((0, 0), (0, n_sub - 1), (0, 0), (0, 0), (0, 0)))
        blk = jnp.concatenate([ch[:, j:j + nchunk] for j in range(n_sub)], axis=2)
        return jnp.transpose(blk, (0, 3, 1, 2, 4)).reshape(b * g, nchunk, NSA_CMP_LEN * dh)

    flat = jnp.stack([flat_blocks(kc), flat_blocks(vc)])
    pe = jnp.stack([pe_k.reshape(1, -1), pe_v.reshape(1, -1)])
    cmp_kv = _nsa_compress(flat, pe, jnp.stack([ck_w1, cv_w1]), jnp.stack([ck_w2, cv_w2]))
    cmp_kv = cmp_kv.reshape(2, b, g, nchunk, dh)
    kct = jnp.transpose(cmp_kv[0], (0, 1, 3, 2)).astype(BF16)
    vcm = cmp_kv[1].astype(BF16)

    cmp_start = jnp.arange(nchunk) * NSA_CMP_STRIDE
    cmp_end = cmp_start + NSA_CMP_LEN - 1
    sel_start = jnp.arange(n_sel) * NSA_SEL_BLOCK
    ovl = ((cmp_end[:, None] >= sel_start[None, :])
           & (cmp_start[:, None] <= sel_start[None, :] + NSA_SEL_BLOCK - 1)
           & (jnp.arange(nchunk)[:, None] < n_cmp)).astype(BF16)

    b0, b1, far = _bias_tiles(tbl, g, hg)
    gp = jnp.pad(gates.reshape(b, s, g, hg * 3), ((0, 0), (0, 0), (0, 0), (0, 128 - hg * 3)))
    gp = jnp.transpose(gp, (0, 2, 1, 3))

    kv_spec_t = pl.BlockSpec((1, 1, nt, dh, Q_BLOCK), lambda bi, gi, i: (bi, gi, 0, 0, 0))
    kv_spec_v = pl.BlockSpec((1, 1, nt, Q_BLOCK, dh), lambda bi, gi, i: (bi, gi, 0, 0, 0))
    bias_spec = pl.BlockSpec((1, hg, Q_BLOCK, Q_BLOCK), lambda bi, gi, i: (gi, 0, 0, 0))
    return pl.pallas_call(
        functools.partial(_nsa_kernel, n_sel=n_sel, n_cmp_pad=nchunk),
        grid=(b, g, nt),
        in_specs=[pl.BlockSpec((1, 1, hg, Q_BLOCK, dh), lambda bi, gi, i: (bi, gi, 0, i, 0)),
                  pl.BlockSpec((1, 1, Q_BLOCK, 128), lambda bi, gi, i: (bi, gi, i, 0)),
                  pl.BlockSpec((1, 1, dh, nchunk), lambda bi, gi, i: (bi, gi, 0, 0)),
                  pl.BlockSpec((1, 1, nchunk, dh), lambda bi, gi, i: (bi, gi, 0, 0)),
                  kv_spec_t, kv_spec_v, kv_spec_t, kv_spec_v,
                  pl.BlockSpec((nchunk, n_sel), lambda bi, gi, i: (0, 0)),
                  bias_spec, bias_spec,
                  pl.BlockSpec((1, hg, 8, Q_BLOCK), lambda bi, gi, i: (gi, 0, 0, 0))],
        out_specs=pl.BlockSpec((1, Q_BLOCK, hg * dh), lambda bi, gi, i: (bi, i, gi)),
        out_shape=jax.ShapeDtypeStruct((b, s, g * hg * dh), F32),
        scratch_shapes=[pltpu.VMEM((hg, Q_BLOCK, 1), F32), pltpu.VMEM((hg, Q_BLOCK, 1), F32),
                        pltpu.VMEM((hg, Q_BLOCK, dh), F32)],
        compiler_params=_params("parallel", "parallel", "arbitrary"),
        name="nsa_attention",
    )(_heads_q(q, b, s, g, hg), gp, kct, vcm,
      _tiles_kt(ks, b, s, g), _tiles_v(vs, b, s, g), _tiles_kt(kw, b, s, g), _tiles_v(vw, b, s, g),
      ovl, b0, b1, far)


def _swa_kernel(q_ref, kt_ref, v_ref, b0_ref, b1_ref, sink_ref, o_ref, m_ref, l_ref, acc_ref):
    qb = Q_BLOCK
    i = pl.program_id(2)
    row = lax.broadcasted_iota(jnp.int32, (qb, qb), 0)
    col = lax.broadcasted_iota(jnp.int32, (qb, qb), 1)
    qs = [q_ref[0, 0, h] for h in range(SWA_HG)]
    for h in range(SWA_HG):
        m_ref[h] = sink_ref[0, h, :, 0:1]
    l_ref[...] = jnp.ones_like(l_ref)
    acc_ref[...] = jnp.zeros_like(acc_ref)

    @pl.when(i >= 1)
    def _():
        k_t = kt_ref[0, 0, i - 1]
        v = v_ref[0, 0, i - 1]
        for h in range(SWA_HG):
            _attn_update(h, qs[h], k_t, v, b1_ref[0, h], row < col, m_ref, l_ref, acc_ref)

    k_t = kt_ref[0, 0, i]
    v = v_ref[0, 0, i]
    for h in range(SWA_HG):
        _attn_update(h, qs[h], k_t, v, b0_ref[0, h], col <= row, m_ref, l_ref, acc_ref)
    o_ref[0] = jnp.concatenate([acc_ref[h] / l_ref[h] for h in range(SWA_HG)], axis=-1)


def _swa(q, k, v, sinks, tbl):
    b, s, _ = q.shape
    g, hg, dh = SWA_KV_HEADS, SWA_HG, HEAD_DIM
    nt = s // Q_BLOCK
    assert SWA_WINDOW == Q_BLOCK
    b0, b1, _ = _bias_tiles(tbl, g, hg)
    sink = jnp.broadcast_to(sinks.reshape(g, hg, 1, 1).astype(F32), (g, hg, Q_BLOCK, 128))
    kv_spec_t = pl.BlockSpec((1, 1, nt, dh, Q_BLOCK), lambda bi, gi, i: (bi, gi, 0, 0, 0))
    kv_spec_v = pl.BlockSpec((1, 1, nt, Q_BLOCK, dh), lambda bi, gi, i: (bi, gi, 0, 0, 0))
    bias_spec = pl.BlockSpec((1, hg, Q_BLOCK, Q_BLOCK), lambda bi, gi, i: (gi, 0, 0, 0))
    return pl.pallas_call(
        _swa_kernel,
        grid=(b, g, nt),
        in_specs=[pl.BlockSpec((1, 1, hg, Q_BLOCK, dh), lambda bi, gi, i: (bi, gi, 0, i, 0)),
                  kv_spec_t, kv_spec_v, bias_spec, bias_spec,
                  pl.BlockSpec((1, hg, Q_BLOCK, 128), lambda bi, gi, i: (gi, 0, 0, 0))],
        out_specs=pl.BlockSpec((1, Q_BLOCK, hg * dh), lambda bi, gi, i: (bi, i, gi)),
        out_shape=jax.ShapeDtypeStruct((b, s, g * hg * dh), F32),
        scratch_shapes=[pltpu.VMEM((hg, Q_BLOCK, 1), F32), pltpu.VMEM((hg, Q_BLOCK, 1), F32),
                        pltpu.VMEM((hg, Q_BLOCK, dh), F32)],
        compiler_params=_params("parallel", "parallel", "arbitrary"),
        name="swa_sink_attention",
    )(_heads_q(q, b, s, g, hg), _tiles_kt(k, b, s, g), _tiles_v(v, b, s, g), b0, b1, sink)


def _merge_kernel(x_ref, oa_ref, ob_ref, oc_ref, ga_ref, gb_ref, gc_ref,
                  wa_ref, wb_ref, wc_ref, wo_ref, o_ref):
    merged = (_sigmoid(ga_ref[...]) * _dot(oa_ref[...], wa_ref[...])
              + _sigmoid(gb_ref[...]) * _dot(ob_ref[...], wb_ref[...])
              + _sigmoid(gc_ref[...]) * _dot(oc_ref[...], wc_ref[...]))
    o_ref[...] = x_ref[...] + _dot(merged, wo_ref[...])


def _merge(x, o_a, o_b, o_c, g_a, g_b, g_c, w_a, w_b, w_c, w_o, tm):
    t, d = x.shape
    tok = lambda w: pl.BlockSpec((tm, w), lambda i: (i, 0))
    full = lambda a: pl.BlockSpec(a.shape, lambda i: (0, 0))
    ws = [w.astype(BF16) for w in (w_a, w_b, w_c, w_o)]
    return pl.pallas_call(
        _merge_kernel,
        grid=(t // tm,),
        in_specs=[tok(d), tok(o_a.shape[1]), tok(o_b.shape[1]), tok(o_c.shape[1]), tok(d), tok(d), tok(d)]
                 + [full(w) for w in ws],
        out_specs=tok(d),
        out_shape=jax.ShapeDtypeStruct((t, d), F32),
        compiler_params=_params("parallel"),
        name="branch_merge",
    )(x, o_a, o_b, o_c, g_a, g_b, g_c, *ws)


def _top_rows(x, k):
    r = x.shape[0]
    ridx = lax.broadcasted_iota(jnp.int32, x.shape, 0)
    vals = []
    for _ in range(k):
        mx = jnp.max(x, axis=0, keepdims=True)
        first = jnp.min(jnp.where(x == mx, ridx, r), axis=0, keepdims=True)
        x = jnp.where(ridx == first, -jnp.inf, x)
        vals.append(mx)
    return jnp.concatenate(vals, axis=0)


def _peer_score_kernel(x_ref, g_ref, wqt_ref, sk_ref, z_ref, s1_ref, s2_ref, e1_ref, e2_ref, tau_ref):
    kk = PEER_TOPK
    half = PEER_QDIM // 2
    zb = _rms(x_ref[...], g_ref[...]).astype(BF16)
    z_ref[...] = zb
    q_t = lax.dot_general(wqt_ref[...], zb, (((1,), (1,)), ((), ())), preferred_element_type=F32)
    q_t = q_t.astype(BF16)
    for h in range(PEER_HEADS):
        s = [jnp.dot(sk_ref[h, p], q_t[(2 * h + p) * half:(2 * h + p + 1) * half, :],
                     preferred_element_type=F32) for p in range(2)]
        top = [_top_rows(s[p], kk) for p in range(2)]
        cand = jnp.concatenate([top[0][i:i + 1] + top[1] for i in range(kk)], axis=0)
        best = _top_rows(cand, kk)
        zsum = jnp.sum(jnp.exp(best - best[0:1]), axis=0, keepdims=True)
        s1_ref[h] = s[0]
        s2_ref[h] = s[1]
        e1_ref[h] = jnp.exp(s[0] - top[0][0:1]) / zsum
        e2_ref[h] = jnp.exp(s[1] - top[1][0:1])
        tau_ref[h] = jnp.broadcast_to(best[kk - 1:kk], (8, best.shape[1]))


def _peer_scores(x, g, wq, subkeys, tt):
    t, d = x.shape
    hp, nk = PEER_HEADS, PEER_NKEYS
    wqt = jnp.transpose(wq).astype(BF16)
    sk = subkeys.astype(BF16)
    stat = lambda rows: pl.BlockSpec((hp, rows, tt), lambda i: (0, 0, i))
    shp = lambda rows: jax.ShapeDtypeStruct((hp, rows, t), F32)
    return pl.pallas_call(
        _peer_score_kernel,
        grid=(t // tt,),
        in_specs=[pl.BlockSpec((tt, d), lambda i: (i, 0)),
                  pl.BlockSpec((1, d), lambda i: (0, 0)),
                  pl.BlockSpec(wqt.shape, lambda i: (0, 0)),
                  pl.BlockSpec(sk.shape, lambda i: (0, 0, 0, 0))],
        out_specs=[pl.BlockSpec((tt, d), lambda i: (i, 0)), stat(nk), stat(nk), stat(nk), stat(nk), stat(8)],
        out_shape=[jax.ShapeDtypeStruct((t, d), BF16), shp(nk), shp(nk), shp(nk), shp(nk), shp(8)],
        compiler_params=_params("parallel"),
        name="peer_scores",
    )(x, g.reshape(1, d), wqt, sk)


def _peer_main_kernel(x_ref, z_ref, s1_ref, s2_ref, e1_ref, e2_ref, tau_ref, u_ref, v_ref, lnf_ref,
                      o_ref, acc_ref, *, rows_per_tile, final_norm):
    e = pl.program_id(1)

    @pl.when(e == 0)
    def _():
        acc_ref[...] = jnp.zeros_like(acc_ref)

    h_t = _dot_nt(u_ref[...], z_ref[...])
    parts = []
    for r in range(rows_per_tile):
        n1 = e * rows_per_tile + r
        w = jnp.zeros(s2_ref.shape[1:], F32)
        for h in range(PEER_HEADS):
            pair = s2_ref[h] + s1_ref[h, pl.ds(n1, 1), :]
            val = e2_ref[h] * e1_ref[h, pl.ds(n1, 1), :]
            w = w + jnp.where(pair >= tau_ref[h, 0:1, :], val, 0.0)
        parts.append(w)
    act = jnp.concatenate(parts, axis=0) * _gelu_exact(h_t)
    acc_ref[...] += _dot_tn(act, v_ref[...])

    @pl.when(e == pl.num_programs(1) - 1)
    def _():
        y = x_ref[...] + acc_ref[...]
        if final_norm:
            y = _rms(y, lnf_ref[...])
        o_ref[...] = y


def _peer_main(x, z, s1, s2, e1, e2, tau, u_tab, v_tab, lnf_w, final_norm, tt, te):
    t, d = x.shape
    hp, nk = PEER_HEADS, PEER_NKEYS
    n_exp = u_tab.shape[0]
    stat = lambda rows: pl.BlockSpec((hp, rows, tt), lambda i, e: (0, 0, i))
    return pl.pallas_call(
        functools.partial(_peer_main_kernel, rows_per_tile=te // nk, final_norm=final_norm),
        grid=(t // tt, n_exp // te),
        in_specs=[pl.BlockSpec((tt, d), lambda i, e: (i, 0)),
                  pl.BlockSpec((tt, d), lambda i, e: (i, 0)),
                  stat(nk), stat(nk), stat(nk), stat(nk), stat(8),
                  pl.BlockSpec((te, d), lambda i, e: (e, 0)),
                  pl.BlockSpec((te, d), lambda i, e: (e, 0)),
                  pl.BlockSpec((1, d), lambda i, e: (0, 0))],
        out_specs=pl.BlockSpec((tt, d), lambda i, e: (i, 0)),
        out_shape=jax.ShapeDtypeStruct((t, d), F32),
        scratch_shapes=[pltpu.VMEM((tt, d), F32)],
        compiler_params=_params("parallel", "arbitrary"),
        name="peer_dense",
    )(x, z, s1, s2, e1, e2, tau, u_tab.astype(BF16), v_tab.astype(BF16), lnf_w.reshape(1, d))


def _pad_cols(w, mult):
    pad = (-w.shape[1]) % mult
    return jnp.pad(w, ((0, 0), (0, pad))) if pad else w


def _split(z, sizes):
    out, o = [], 0
    for sz in sizes:
        out.append(z[..., o:o + sz])
        o += sz
    return out


def _layer(x, b, s, p, l, nsa_tbl, swa_tbl, lnf_w, final_norm):
    t, d = x.shape
    hd = HEAD_DIM
    proj = _rms_matmul(x, p["ln1_w"][l], _pad_cols(p["w_in"][l], 512).astype(BF16), 512, 512)
    nsa_sizes = (NSA_Q_HEADS * hd,) + (NSA_KV_GROUPS * hd,) * 6 + (NSA_Q_HEADS * 3,)
    swa_sizes = (SWA_Q_HEADS * hd, SWA_KV_HEADS * hd, SWA_KV_HEADS * hd)
    cols = _split(proj, (RWKV_COLS,) + nsa_sizes + swa_sizes + (d, d, d))
    rw = cols[0].reshape(b, s, RWKV_COLS)
    rw_prev = jnp.pad(rw, ((0, 0), (1, 0), (0, 0)))[:, :-1]
    rw = rw + p["rwkv_mu"][l] * (rw_prev - rw)
    o_a = _rwkv(rw, p["rwkv_w0"][l], p["rwkv_w2"][l], p["rwkv_a0"][l], p["rwkv_a2"][l], p["rwkv_g2"][l],
                p["rwkv_k_k"][l], p["rwkv_k_a"][l], p["rwkv_r_k"][l], p["rwkv_ln_w"][l], p["rwkv_ln_b"][l])
    seq = lambda z: z.reshape(b, s, z.shape[-1])
    nq, nkc, nvc, nks, nvs, nkw, nvw, ngate = [seq(c) for c in cols[1:9]]
    o_b = _nsa(nq, nkc, nvc, nks, nvs, nkw, nvw, ngate, p["nsa_pe_k"][l], p["nsa_pe_v"][l],
               p["nsa_ck_w1"][l], p["nsa_ck_w2"][l], p["nsa_cv_w1"][l], p["nsa_cv_w2"][l], nsa_tbl)
    sq, sk, sv = [seq(c) for c in cols[9:12]]
    o_c = _swa(sq, sk, sv, p["swa_sinks"][l], swa_tbl)
    g_a, g_b, g_c = cols[12:15]
    x = _merge(x, o_a.reshape(t, -1), o_b.reshape(t, -1), o_c.reshape(t, -1), g_a, g_b, g_c,
               p["w_br_a"][l], p["w_br_b"][l], p["w_br_c"][l], p["w_out"][l], 256)
    z, s1, s2, e1, e2, tau = _peer_scores(x, p["ln2_w"][l], p["peer_wq"][l], p["peer_subkeys"][l], 256)
    return _peer_main(x, z, s1, s2, e1, e2, tau, p["peer_u"][l], p["peer_v"][l], lnf_w, final_norm, 512, 512)


def kernel(x, ln1_w, ln2_w, lnf_w, rel_bias, w_in, rwkv_mu, rwkv_w0, rwkv_w2, rwkv_a0, rwkv_a2, rwkv_g2, rwkv_k_k, rwkv_k_a, rwkv_r_k, rwkv_ln_w, rwkv_ln_b, nsa_pe_k, nsa_pe_v, nsa_ck_w1, nsa_ck_w2, nsa_cv_w1, nsa_cv_w2, swa_sinks, w_br_a, w_br_b, w_br_c, w_out, peer_wq, peer_subkeys, peer_u, peer_v):
    p = dict(ln1_w=ln1_w, ln2_w=ln2_w, w_in=w_in, rwkv_mu=rwkv_mu, rwkv_w0=rwkv_w0, rwkv_w2=rwkv_w2,
             rwkv_a0=rwkv_a0, rwkv_a2=rwkv_a2, rwkv_g2=rwkv_g2, rwkv_k_k=rwkv_k_k, rwkv_k_a=rwkv_k_a,
             rwkv_r_k=rwkv_r_k.reshape(rwkv_r_k.shape[0], -1), rwkv_ln_w=rwkv_ln_w, rwkv_ln_b=rwkv_ln_b,
             nsa_pe_k=nsa_pe_k, nsa_pe_v=nsa_pe_v, nsa_ck_w1=nsa_ck_w1, nsa_ck_w2=nsa_ck_w2,
             nsa_cv_w1=nsa_cv_w1, nsa_cv_w2=nsa_cv_w2, swa_sinks=swa_sinks, w_br_a=w_br_a, w_br_b=w_br_b,
             w_br_c=w_br_c, w_out=w_out, peer_wq=peer_wq, peer_subkeys=peer_subkeys, peer_u=peer_u,
             peer_v=peer_v)
    b, s, d = x.shape
    depth = w_in.shape[0]
    nsa_tbl = rel_bias[:, :NSA_Q_HEADS]
    swa_tbl = rel_bias[:, NSA_Q_HEADS:]
    y = x.reshape(b * s, d)
    for l in range(depth):
        y = _layer(y, b, s, p, l, nsa_tbl, swa_tbl, lnf_w, l == depth - 1)
    return y.reshape(b, s, d)
```

```python
import functools
import math

import jax
import jax.numpy as jnp
from jax import lax
from jax.experimental import pallas as pl
from jax.experimental.pallas import tpu as pltpu

F32 = jnp.float32
BF16 = jnp.bfloat16

D_MODEL = 1024
HEAD_DIM = 64
Q_BLOCK = 128
NORM_EPS = 1e-6

RWKV_HEADS = 8
RWKV_WIDTH = RWKV_HEADS * HEAD_DIM
RWKV_DECAY_RANK = 64
RWKV_ICLR_RANK = 64
RWKV_GATE_RANK = 128
RWKV_GN_EPS = 64e-5
RWKV_COLS = 3 * RWKV_WIDTH + RWKV_DECAY_RANK + RWKV_ICLR_RANK + RWKV_GATE_RANK
RWKV_CHUNK = 64

NSA_Q_HEADS = 8
NSA_KV_GROUPS = 2
NSA_HG = NSA_Q_HEADS // NSA_KV_GROUPS
NSA_CMP_LEN = 32
NSA_CMP_STRIDE = 16
NSA_CMP_HIDDEN = 256
NSA_SEL_BLOCK = 64
NSA_SEL_TOPN = 16
NSA_WINDOW = 512
NSA_FORCE_SCORE = 1e9

SWA_Q_HEADS = 8
SWA_KV_HEADS = 2
SWA_HG = SWA_Q_HEADS // SWA_KV_HEADS
SWA_WINDOW = 128

REL_BUCKETS = 32
REL_MAX_DIST = 128

PEER_HEADS = 8
PEER_NKEYS = 128
PEER_QDIM = 256
PEER_TOPK = 16

NEG_BIG = -1e30
VMEM_LIMIT_BYTES = 56 * 1024 * 1024


def _params(*sem):
    return pltpu.CompilerParams(dimension_semantics=sem, vmem_limit_bytes=VMEM_LIMIT_BYTES)


def _dot(a, b):
    return jnp.dot(a.astype(BF16), b.astype(BF16), preferred_element_type=F32)


def _dot_nt(a, b):
    return lax.dot_general(a.astype(BF16), b.astype(BF16), (((1,), (1,)), ((), ())),
                           preferred_element_type=F32)


def _dot_tn(a, b):
    return lax.dot_general(a.astype(BF16), b.astype(BF16), (((0,), (0,)), ((), ())),
                           preferred_element_type=F32)


def _split_dot(a, b_exact):
    hi = a.astype(BF16)
    lo = (a - hi.astype(F32)).astype(BF16)
    return (jnp.dot(hi, b_exact, preferred_element_type=F32)
            + jnp.dot(lo, b_exact, preferred_element_type=F32))


def _split_dot_rhs(a_exact, b):
    hi = b.astype(BF16)
    lo = (b - hi.astype(F32)).astype(BF16)
    return (jnp.dot(a_exact, hi, preferred_element_type=F32)
            + jnp.dot(a_exact, lo, preferred_element_type=F32))


def _sigmoid(x):
    return 1.0 / (1.0 + jnp.exp(-x))


def _gelu_exact(x):
    return 0.5 * x * (1.0 + lax.erf(x * (1.0 / math.sqrt(2.0))))


def _rms(x, g):
    return x * lax.rsqrt(jnp.mean(x * x, axis=-1, keepdims=True) + NORM_EPS) * g


def _rms_matmul_kernel(x_ref, g_ref, w_ref, o_ref):
    y = _rms(x_ref[...], g_ref[...])
    o_ref[...] = jnp.dot(y.astype(BF16), w_ref[...], preferred_element_type=F32)


def _rms_matmul(x, g, w, tm, tn):
    t, d = x.shape
    n = w.shape[1]
    return pl.pallas_call(
        _rms_matmul_kernel,
        grid=(t // tm, n // tn),
        in_specs=[pl.BlockSpec((tm, d), lambda i, j: (i, 0)),
                  pl.BlockSpec((1, d), lambda i, j: (0, 0)),
                  pl.BlockSpec((d, tn), lambda i, j: (0, j))],
        out_specs=pl.BlockSpec((tm, tn), lambda i, j: (i, j)),
        out_shape=jax.ShapeDtypeStruct((t, n), F32),
        compiler_params=_params("parallel", "arbitrary"),
        name="rms_proj",
    )(x, g.reshape(1, d), w)


def _rwkv_kernel(z_ref, w0_ref, a0_ref, kk_ref, ka_ref, rk_ref, lnw_ref, lnb_ref,
                 w2_ref, a2_ref, g2_ref, tri_ref, bd_ref, o_ref, st_ref):
    c = RWKV_CHUNK
    n = HEAD_DIM
    w_ = RWKV_WIDTH

    @pl.when(pl.program_id(1) == 0)
    def _():
        st_ref[...] = jnp.zeros_like(st_ref)

    z = z_ref[0]
    zr = z[:, 0:w_]
    zk = z[:, w_:2 * w_]
    zv = z[:, 2 * w_:3 * w_]
    o1 = 3 * w_
    zw = z[:, o1:o1 + RWKV_DECAY_RANK]
    za = z[:, o1 + RWKV_DECAY_RANK:o1 + RWKV_DECAY_RANK + RWKV_ICLR_RANK]
    zg = z[:, o1 + RWKV_DECAY_RANK + RWKV_ICLR_RANK:]
    bd = bd_ref[...]
    tri = tri_ref[...]

    y = -(w0_ref[...] + _dot(jnp.tanh(zw), w2_ref[...]))
    softplus = jnp.maximum(y, 0.0) + jnp.log(1.0 + jnp.exp(-jnp.abs(y)))
    lw = -jnp.exp(-softplus - 0.5)
    a = _sigmoid(a0_ref[...] + _dot(za, a2_ref[...]))
    g = _dot(_sigmoid(zg), g2_ref[...])
    kk = zk * kk_ref[...]
    kk = kk * lax.rsqrt(jnp.maximum(_split_dot(kk * kk, bd), 1e-12))
    k = zk * (1.0 + (a - 1.0) * ka_ref[...])

    cum = _split_dot_rhs(tri, lw)
    e_pos = jnp.exp(cum)
    e_neg = jnp.exp(-cum)
    a_t = -kk * jnp.exp(cum - lw)
    b_t = kk * a * e_neg
    k_t = k * e_neg
    r_t = zr * e_pos

    ri = lax.broadcasted_iota(jnp.int32, (c, c), 0)
    ci = lax.broadcasted_iota(jnp.int32, (c, c), 1)
    strict = ri > ci
    incl = ri >= ci
    eye = (ri == ci).astype(F32)

    heads = range(RWKV_HEADS)
    sl = [slice(h * n, (h + 1) * n) for h in heads]
    ah = [a_t[:, s] for s in sl]
    bh = [b_t[:, s] for s in sl]
    kh = [k_t[:, s] for s in sl]
    rh = [r_t[:, s] for s in sl]
    vh = [zv[:, s] for s in sl]
    s0 = [st_ref[h] for h in heads]
    p = [jnp.where(strict, _dot_nt(ah[h], bh[h]), 0.0) for h in heads]
    l_ak = [jnp.where(strict, _dot_nt(ah[h], kh[h]), 0.0) for h in heads]
    m_rb = [jnp.where(incl, _dot_nt(rh[h], bh[h]), 0.0) for h in heads]
    m_rk = [jnp.where(incl, _dot_nt(rh[h], kh[h]), 0.0) for h in heads]
    rhs = [_dot_nt(ah[h], s0[h]) + _dot(l_ak[h], vh[h]) for h in heads]
    y0 = [_dot_nt(rh[h], s0[h]) + _dot(m_rk[h], vh[h]) for h in heads]
    s1 = [s0[h] + _dot_tn(vh[h], kh[h]) for h in heads]
    tinv = [eye + p[h] for h in heads]
    for _ in range(int(math.log2(c)) - 1):
        p = [_dot(p[h], p[h]) for h in heads]
        tinv = [tinv[h] + _dot(p[h], tinv[h]) for h in heads]
    u = [_dot(tinv[h], rhs[h]) for h in heads]
    ys = [y0[h] + _dot(m_rb[h], u[h]) for h in heads]
    for h in heads:
        st_ref[h] = (s1[h] + _dot_tn(u[h], bh[h])) * e_pos[c - 1:c, sl[h]]
    yv = jnp.concatenate(ys, axis=-1)

    mu = _split_dot(yv, bd) * (1.0 / n)
    dlt = yv - mu
    var = _split_dot(dlt * dlt, bd) * (1.0 / n)
    yn = dlt * lax.rsqrt(var + RWKV_GN_EPS) * lnw_ref[...] + lnb_ref[...]
    bonus = _split_dot(zr * k * rk_ref[...], bd) * zv
    o_ref[0] = (yn + bonus) * g


def _rwkv(rw, w0, w2, a0, a2, g2, k_k, k_a, r_k, ln_w, ln_b):
    b, s, cols = rw.shape
    c = RWKV_CHUNK
    w_ = RWKV_WIDTH
    row = lambda v: v.reshape(1, w_).astype(F32)
    idx = jnp.arange(w_) // HEAD_DIM
    bd = (idx[:, None] == idx[None, :]).astype(BF16)
    tri = (jnp.arange(c)[:, None] >= jnp.arange(c)[None, :]).astype(BF16)
    full = lambda shp: pl.BlockSpec(shp, lambda i, j: (0,) * len(shp))
    return pl.pallas_call(
        _rwkv_kernel,
        grid=(b, s // c),
        in_specs=[pl.BlockSpec((1, c, cols), lambda i, j: (i, j, 0))]
                 + [full((1, w_))] * 7
                 + [full((RWKV_DECAY_RANK, w_)), full((RWKV_ICLR_RANK, w_)), full((RWKV_GATE_RANK, w_)),
                    full((c, c)), full((w_, w_))],
        out_specs=pl.BlockSpec((1, c, w_), lambda i, j: (i, j, 0)),
        out_shape=jax.ShapeDtypeStruct((b, s, w_), F32),
        scratch_shapes=[pltpu.VMEM((RWKV_HEADS, HEAD_DIM, HEAD_DIM), F32)],
        compiler_params=_params("parallel", "arbitrary"),
        name="rwkv7_chunked",
    )(rw, row(w0), row(a0), row(k_k), row(k_a), row(r_k), row(ln_w), row(ln_b),
      w2.astype(BF16), a2.astype(BF16), g2.astype(BF16), tri, bd)


def _cmp_kernel(f_ref, pe_ref, w1_ref, w2_ref, o_ref):
    f = f_ref[0, 0] + pe_ref[0]
    h = _gelu_exact(_dot(f, w1_ref[0]))
    o_ref[0, 0] = _dot(h, w2_ref[0])


def _nsa_compress(flat, pe, w1, w2):
    two, bg, ncp, lin = flat.shape
    hid = w1.shape[-1]
    return pl.pallas_call(
        _cmp_kernel,
        grid=(two, bg),
        in_specs=[pl.BlockSpec((1, 1, ncp, lin), lambda i, j: (i, j, 0, 0)),
                  pl.BlockSpec((1, 1, lin), lambda i, j: (i, 0, 0)),
                  pl.BlockSpec((1, lin, hid), lambda i, j: (i, 0, 0)),
                  pl.BlockSpec((1, hid, HEAD_DIM), lambda i, j: (i, 0, 0))],
        out_specs=pl.BlockSpec((1, 1, ncp, HEAD_DIM), lambda i, j: (i, j, 0, 0)),
        out_shape=jax.ShapeDtypeStruct((two, bg, ncp, HEAD_DIM), F32),
        compiler_params=_params("parallel", "parallel"),
        name="nsa_compress",
    )(flat, pe, w1.astype(BF16), w2.astype(BF16))


def _attn_step(k_tile, vt_tile, qt, bias, m_ref, l_ref, acc_ref):
    s = jnp.dot(k_tile, qt, preferred_element_type=F32) + bias
    m_old = m_ref[0:1, :]
    m_new = jnp.maximum(m_old, jnp.max(s, axis=0, keepdims=True))
    p = jnp.exp(s - m_new)
    alpha = jnp.exp(m_old - m_new)
    l_ref[0:1, :] = alpha * l_ref[0:1, :] + jnp.sum(p, axis=0, keepdims=True)
    acc_ref[...] = alpha * acc_ref[...] + jnp.dot(vt_tile, p.astype(BF16), preferred_element_type=F32)
    m_ref[0:1, :] = m_new


def _key_rows(ref, j):
    return ref[0, 0, pl.ds(pl.multiple_of(j * Q_BLOCK, Q_BLOCK), Q_BLOCK), :]


def _nsa_kernel(qt_ref, gate_ref, kc_ref, vct_ref, ks_ref, vst_ref, kw_ref, vwt_ref, ovlt_ref,
                d0_ref, d1_ref, edge_ref, far_ref, o_ref, sb_ref, m_ref, l_ref, acc_ref,
                *, n_sel, n_cmp_pad):
    qb = Q_BLOCK
    hg = NSA_HG
    i = pl.program_id(2)
    s0 = i * qb
    qt = qt_ref[0, 0, 0]

    crow = lax.broadcasted_iota(jnp.int32, (n_cmp_pad, qb), 0)
    tcol = s0 + lax.broadcasted_iota(jnp.int32, (n_cmp_pad, qb), 1)
    cmask = (NSA_CMP_STRIDE * crow + NSA_CMP_LEN - 1) <= tcol
    st = jnp.dot(kc_ref[0, 0], qt, preferred_element_type=F32)
    vct = vct_ref[0, 0]
    psum = jnp.zeros((n_cmp_pad, qb), F32)
    o_cmp = []
    for h in range(hg):
        s = jnp.where(cmask, st[:, h * qb:(h + 1) * qb], NEG_BIG)
        mx = jnp.max(s, axis=0, keepdims=True)
        e = jnp.where(cmask, jnp.exp(s - mx), 0.0)
        p = e / jnp.maximum(jnp.sum(e, axis=0, keepdims=True), 1e-30)
        o_cmp.append(jnp.dot(vct, p.astype(BF16), preferred_element_type=F32))
        psum = psum + p

    imp = _split_dot_rhs(ovlt_ref[...], psum)
    nrow = lax.broadcasted_iota(jnp.int32, (n_sel, qb), 0)
    cur = lax.shift_right_logical(s0 + lax.broadcasted_iota(jnp.int32, (n_sel, qb), 1),
                                  int(math.log2(NSA_SEL_BLOCK)))
    forced = (nrow == 0) | (nrow == cur) | (nrow == cur - 1)
    score = jnp.where(forced, NSA_FORCE_SCORE, jnp.where(nrow <= cur, imp, -1.0))
    sb = jnp.full((n_sel, qb), NEG_BIG, F32)
    for _ in range(min(NSA_SEL_TOPN, n_sel)):
        mx = jnp.max(score, axis=0, keepdims=True)
        first = jnp.min(jnp.where(score == mx, nrow, n_sel), axis=0, keepdims=True)
        hit = nrow == first
        sb = jnp.where(hit, 0.0, sb)
        score = jnp.where(hit, -jnp.inf, score)
    sb_ref[...] = sb

    blocks_per_tile = qb // NSA_SEL_BLOCK

    def sel_bias(j):
        rows = [jnp.broadcast_to(sb_ref[pl.ds(blocks_per_tile * j + r, 1), :], (NSA_SEL_BLOCK, qb))
                for r in range(blocks_per_tile)]
        return jnp.concatenate([jnp.concatenate(rows, axis=0)] * hg, axis=1)

    def reset():
        m_ref[...] = jnp.full_like(m_ref, NEG_BIG)
        l_ref[...] = jnp.zeros_like(l_ref)
        acc_ref[...] = jnp.zeros_like(acc_ref)

    def finish():
        return acc_ref[...] / jnp.maximum(l_ref[0:1, :], 1e-30)

    far = far_ref[0, 0:1, :]

    def sel_step(j, tile_bias):
        _attn_step(_key_rows(ks_ref, j), vst_ref[0, 0, j], qt, sel_bias(j) + tile_bias,
                   m_ref, l_ref, acc_ref)

    reset()
    sel_step(i, d0_ref[0])

    @pl.when(i >= 1)
    def _():
        sel_step(i - 1, d1_ref[0])

    def far_body(j, carry):
        sel_step(j, far)
        return carry

    lax.fori_loop(0, jnp.maximum(i - 1, 0), far_body, 0)
    o_sel = finish()

    reset()
    n_back = NSA_WINDOW // qb
    for d in range(n_back + 1):
        if d == 0:
            tile_bias = d0_ref[0]
        elif d == 1:
            tile_bias = d1_ref[0]
        elif d == n_back:
            tile_bias = edge_ref[...] + far
        else:
            tile_bias = far

        def win_step(d=d, tile_bias=tile_bias):
            _attn_step(_key_rows(kw_ref, i - d), vwt_ref[0, 0, i - d], qt, tile_bias, m_ref, l_ref, acc_ref)

        if d == 0:
            win_step()
        else:
            pl.when(i >= d)(win_step)
    o_win = finish()

    gs = _sigmoid(gate_ref[0, 0, 0])
    o_ref[0, 0, 0] = (gs[0:1, :] * jnp.concatenate(o_cmp, axis=1) + gs[1:2, :] * o_sel + gs[2:3, :] * o_win)


def _t5_bucket(dist):
    n = jnp.maximum(dist, 0)
    max_exact = REL_BUCKETS // 2
    nf = jnp.maximum(n, max_exact).astype(F32)
    large = max_exact + (jnp.log(nf / max_exact) / math.log(REL_MAX_DIST / max_exact)
                         * (REL_BUCKETS - max_exact)).astype(jnp.int32)
    large = jnp.minimum(large, REL_BUCKETS - 1)
    return jnp.where(n < max_exact, n, large)


def _bias_tiles(tbl, groups, hg):
    qb = Q_BLOCK
    kj = jnp.arange(qb)[:, None]
    qi = jnp.arange(qb)[None, :]
    t = tbl.T.reshape(groups, hg, REL_BUCKETS).astype(F32)
    lay = lambda x: jnp.transpose(x, (0, 2, 1, 3)).reshape(groups, qb, hg * qb)
    d0 = lay(t[:, :, _t5_bucket(qi - kj)] + jnp.where(kj <= qi, 0.0, NEG_BIG))
    d1 = lay(t[:, :, _t5_bucket(qi - kj + qb)])
    edge = jnp.tile(jnp.where(qi < kj, 0.0, NEG_BIG).astype(F32), (1, hg))
    far = jnp.broadcast_to(t[:, :, REL_BUCKETS - 1][:, None, :, None], (groups, 8, hg, qb))
    return d0, d1, edge, far.reshape(groups, 8, hg * qb)


def _tiles_t(z, b, s, g):
    nt = s // Q_BLOCK
    z = z.reshape(b, nt, Q_BLOCK, g, HEAD_DIM)
    return jnp.transpose(z, (0, 3, 1, 4, 2)).astype(BF16)


def _rows(z, b, s, g):
    return jnp.transpose(z.reshape(b, s, g, HEAD_DIM), (0, 2, 1, 3)).astype(BF16)


def _q_t(q, b, s, g, hg):
    nq = s // Q_BLOCK
    q = q.reshape(b, nq, Q_BLOCK, g, hg, HEAD_DIM) * (HEAD_DIM ** -0.5)
    return jnp.transpose(q, (0, 3, 1, 5, 4, 2)).reshape(b, g, nq, HEAD_DIM, hg * Q_BLOCK).astype(BF16)


def _o_from_t(o_t, b, s, g, hg):
    nq = s // Q_BLOCK
    o = o_t.reshape(b, g, nq, HEAD_DIM, hg, Q_BLOCK)
    return jnp.transpose(o, (0, 2, 5, 1, 4, 3)).reshape(b, s, g * hg * HEAD_DIM)


def _nsa(q, kc, vc, ks, vs, kw, vw, gates, pe_k, pe_v, ck_w1, ck_w2, cv_w1, cv_w2, tbl):
    b, s, _ = q.shape
    g, hg, dh = NSA_KV_GROUPS, NSA_HG, HEAD_DIM
    qb = Q_BLOCK
    nchunk = s // NSA_CMP_STRIDE
    n_sub = NSA_CMP_LEN // NSA_CMP_STRIDE
    n_cmp = nchunk - n_sub + 1
    n_sel = s // NSA_SEL_BLOCK
    nt = s // qb

    def flat_blocks(z):
        ch = z.reshape(b, nchunk, NSA_CMP_STRIDE, g, dh)
        ch = jnp.pad(ch, ((0, 0), (0, n_sub - 1), (0, 0), (0, 0), (0, 0)))
        blk = jnp.concatenate([ch[:, j:j + nchunk] for j in range(n_sub)], axis=2)
        return jnp.transpose(blk, (0, 3, 1, 2, 4)).reshape(b * g, nchunk, NSA_CMP_LEN * dh)

    flat = jnp.stack([flat_blocks(kc), flat_blocks(vc)])
    pe = jnp.stack([pe_k.reshape(1, -1), pe_v.reshape(1, -1)])
    cmp_kv = _nsa_compress(flat, pe, jnp.stack([ck_w1, cv_w1]), jnp.stack([ck_w2, cv_w2]))
    cmp_kv = cmp_kv.reshape(2, b, g, nchunk, dh)
    kcm = cmp_kv[0].astype(BF16)
    vct = jnp.transpose(cmp_kv[1], (0, 1, 3, 2)).astype(BF16)

    cmp_start = jnp.arange(nchunk) * NSA_CMP_STRIDE
    cmp_end = cmp_start + NSA_CMP_LEN - 1
    sel_start = jnp.arange(n_sel) * NSA_SEL_BLOCK
    ovlt = ((cmp_end[None, :] >= sel_start[:, None])
            & (cmp_start[None, :] <= sel_start[:, None] + NSA_SEL_BLOCK - 1)
            & (jnp.arange(nchunk)[None, :] < n_cmp)).astype(BF16)

    d0, d1, edge, far = _bias_tiles(tbl, g, hg)
    gt = gates.reshape(b, nt, qb, g, hg, 3)
    gt = jnp.transpose(gt, (0, 3, 1, 5, 4, 2)).reshape(b, g, nt, 3, hg * qb)
    gt = jnp.pad(gt, ((0, 0), (0, 0), (0, 0), (0, 5), (0, 0)))

    wide = hg * qb
    per_q = lambda rows: pl.BlockSpec((1, 1, 1, rows, wide), lambda bi, gi, i: (bi, gi, i, 0, 0))
    rows_spec = pl.BlockSpec((1, 1, s, dh), lambda bi, gi, i: (bi, gi, 0, 0))
    tiles_spec = pl.BlockSpec((1, 1, nt, dh, qb), lambda bi, gi, i: (bi, gi, 0, 0, 0))
    bias_spec = pl.BlockSpec((1, qb, wide), lambda bi, gi, i: (gi, 0, 0))
    o_t = pl.pallas_call(
        functools.partial(_nsa_kernel, n_sel=n_sel, n_cmp_pad=nchunk),
        grid=(b, g, nt),
        in_specs=[per_q(dh), per_q(8),
                  pl.BlockSpec((1, 1, nchunk, dh), lambda bi, gi, i: (bi, gi, 0, 0)),
                  pl.BlockSpec((1, 1, dh, nchunk), lambda bi, gi, i: (bi, gi, 0, 0)),
                  rows_spec, tiles_spec, rows_spec, tiles_spec,
                  pl.BlockSpec((n_sel, nchunk), lambda bi, gi, i: (0, 0)),
                  bias_spec, bias_spec,
                  pl.BlockSpec((qb, wide), lambda bi, gi, i: (0, 0)),
                  pl.BlockSpec((1, 8, wide), lambda bi, gi, i: (gi, 0, 0))],
        out_specs=per_q(dh),
        out_shape=jax.ShapeDtypeStruct((b, g, nt, dh, wide), F32),
        scratch_shapes=[pltpu.VMEM((n_sel, qb), F32), pltpu.VMEM((8, wide), F32),
                        pltpu.VMEM((8, wide), F32), pltpu.VMEM((dh, wide), F32)],
        compiler_params=_params("parallel", "parallel", "arbitrary"),
        name="nsa_attention",
    )(_q_t(q, b, s, g, hg), gt, kcm, vct,
      _rows(ks, b, s, g), _tiles_t(vs, b, s, g), _rows(kw, b, s, g), _tiles_t(vw, b, s, g),
      ovlt, d0, d1, edge, far)
    return _o_from_t(o_t, b, s, g, hg)


def _swa_kernel(qt_ref, k_ref, vt_ref, d0_ref, d1_ref, sink_ref, o_ref, m_ref, l_ref, acc_ref):
    i = pl.program_id(2)
    qt = qt_ref[0, 0, 0]
    m_ref[...] = sink_ref[0]
    l_ref[...] = jnp.ones_like(l_ref)
    acc_ref[...] = jnp.zeros_like(acc_ref)
    _attn_step(_key_rows(k_ref, i), vt_ref[0, 0, i], qt, d0_ref[0], m_ref, l_ref, acc_ref)

    @pl.when(i >= 1)
    def _():
        _attn_step(_key_rows(k_ref, i - 1), vt_ref[0, 0, i - 1], qt, d1_ref[0], m_ref, l_ref, acc_ref)

    o_ref[0, 0, 0] = acc_ref[...] / l_ref[0:1, :]


def _swa(q, k, v, sinks, tbl):
    b, s, _ = q.shape
    g, hg, dh = SWA_KV_HEADS, SWA_HG, HEAD_DIM
    qb = Q_BLOCK
    nt = s // qb
    assert SWA_WINDOW == qb
    d0, d1, edge, _ = _bias_tiles(tbl, g, hg)
    d1 = d1 + edge[None]
    wide = hg * qb
    sink = jnp.broadcast_to(sinks.reshape(g, 1, hg, 1).astype(F32), (g, 8, hg, qb)).reshape(g, 8, wide)
    per_q = pl.BlockSpec((1, 1, 1, dh, wide), lambda bi, gi, i: (bi, gi, i, 0, 0))
    bias_spec = pl.BlockSpec((1, qb, wide), lambda bi, gi, i: (gi, 0, 0))
    o_t = pl.pallas_call(
        _swa_kernel,
        grid=(b, g, nt),
        in_specs=[per_q,
                  pl.BlockSpec((1, 1, s, dh), lambda bi, gi, i: (bi, gi, 0, 0)),
                  pl.BlockSpec((1, 1, nt, dh, qb), lambda bi, gi, i: (bi, gi, 0, 0, 0)),
                  bias_spec, bias_spec,
                  pl.BlockSpec((1, 8, wide), lambda bi, gi, i: (gi, 0, 0))],
        out_specs=per_q,
        out_shape=jax.ShapeDtypeStruct((b, g, nt, dh, wide), F32),
        scratch_shapes=[pltpu.VMEM((8, wide), F32), pltpu.VMEM((8, wide), F32), pltpu.VMEM((dh, wide), F32)],
        compiler_params=_params("parallel", "parallel", "arbitrary"),
        name="swa_sink_attention",
    )(_q_t(q, b, s, g, hg), _rows(k, b, s, g), _tiles_t(v, b, s, g), d0, d1, sink)
    return _o_from_t(o_t, b, s, g, hg)


def _merge_kernel(x_ref, oa_ref, ob_ref, oc_ref, ga_ref, gb_ref, gc_ref,
                  wa_ref, wb_ref, wc_ref, wo_ref, o_ref):
    merged = (_sigmoid(ga_ref[...]) * _dot(oa_ref[...], wa_ref[...])
              + _sigmoid(gb_ref[...]) * _dot(ob_ref[...], wb_ref[...])
              + _sigmoid(gc_ref[...]) * _dot(oc_ref[...], wc_ref[...]))
    o_ref[...] = x_ref[...] + _dot(merged, wo_ref[...])


def _merge(x, o_a, o_b, o_c, g_a, g_b, g_c, w_a, w_b, w_c, w_o, tm):
    t, d = x.shape
    tok = lambda w: pl.BlockSpec((tm, w), lambda i: (i, 0))
    full = lambda a: pl.BlockSpec(a.shape, lambda i: (0, 0))
    ws = [w.astype(BF16) for w in (w_a, w_b, w_c, w_o)]
    return pl.pallas_call(
        _merge_kernel,
        grid=(t // tm,),
        in_specs=[tok(d), tok(o_a.shape[1]), tok(o_b.shape[1]), tok(o_c.shape[1]), tok(d), tok(d), tok(d)]
                 + [full(w) for w in ws],
        out_specs=tok(d),
        out_shape=jax.ShapeDtypeStruct((t, d), F32),
        compiler_params=_params("parallel"),
        name="branch_merge",
    )(x, o_a, o_b, o_c, g_a, g_b, g_c, *ws)


def _top_rows(x, k):
    r = x.shape[0]
    ridx = lax.broadcasted_iota(jnp.int32, x.shape, 0)
    vals = []
    for _ in range(k):
        mx = jnp.max(x, axis=0, keepdims=True)
        first = jnp.min(jnp.where(x == mx, ridx, r), axis=0, keepdims=True)
        x = jnp.where(ridx == first, -jnp.inf, x)
        vals.append(mx)
    return jnp.concatenate(vals, axis=0)


def _peer_score_kernel(x_ref, g_ref, wqt_ref, sk_ref, z_ref, s1_ref, s2_ref, e1_ref, e2_ref, tau_ref):
    kk = PEER_TOPK
    half = PEER_QDIM // 2
    zb = _rms(x_ref[...], g_ref[...]).astype(BF16)
    z_ref[...] = zb
    q_t = lax.dot_general(wqt_ref[...], zb, (((1,), (1,)), ((), ())), preferred_element_type=F32)
    q_t = q_t.astype(BF16)
    for h in range(PEER_HEADS):
        s = [jnp.dot(sk_ref[h, p], q_t[(2 * h + p) * half:(2 * h + p + 1) * half, :],
                     preferred_element_type=F32) for p in range(2)]
        top = [_top_rows(s[p], kk) for p in range(2)]
        cand = jnp.concatenate([top[0][i:i + 1] + top[1] for i in range(kk)], axis=0)
        best = _top_rows(cand, kk)
        zsum = jnp.sum(jnp.exp(best - best[0:1]), axis=0, keepdims=True)
        s1_ref[h] = s[0]
        s2_ref[h] = s[1]
        e1_ref[h] = jnp.exp(s[0] - top[0][0:1]) / zsum
        e2_ref[h] = jnp.exp(s[1] - top[1][0:1])
        tau_ref[h] = jnp.broadcast_to(best[kk - 1:kk], (8, best.shape[1]))


def _peer_scores(x, g, wq, subkeys, tt):
    t, d = x.shape
    hp, nk = PEER_HEADS, PEER_NKEYS
    wqt = jnp.transpose(wq).astype(BF16)
    sk = subkeys.astype(BF16)
    stat = lambda rows: pl.BlockSpec((hp, rows, tt), lambda i: (0, 0, i))
    shp = lambda rows: jax.ShapeDtypeStruct((hp, rows, t), F32)
    return pl.pallas_call(
        _peer_score_kernel,
        grid=(t // tt,),
        in_specs=[pl.BlockSpec((tt, d), lambda i: (i, 0)),
                  pl.BlockSpec((1, d), lambda i: (0, 0)),
                  pl.BlockSpec(wqt.shape, lambda i: (0, 0)),
                  pl.BlockSpec(sk.shape, lambda i: (0, 0, 0, 0))],
        out_specs=[pl.BlockSpec((tt, d), lambda i: (i, 0)), stat(nk), stat(nk), stat(nk), stat(nk), stat(8)],
        out_shape=[jax.ShapeDtypeStruct((t, d), BF16), shp(nk), shp(nk), shp(nk), shp(nk), shp(8)],
        compiler_params=_params("parallel"),
        name="peer_scores",
    )(x, g.reshape(1, d), wqt, sk)


def _peer_main_kernel(x_ref, z_ref, s1_ref, s2_ref, e1_ref, e2_ref, tau_ref, u_ref, v_ref, lnf_ref,
                      o_ref, acc_ref, w_ref, *, rows_per_tile, final_norm):
    e = pl.program_id(1)

    @pl.when(e == 0)
    def _():
        acc_ref[...] = jnp.zeros_like(acc_ref)

    h_t = _dot_nt(u_ref[...], z_ref[...])
    nk = PEER_NKEYS
    lane_chunk = 128
    for c0 in range(0, s2_ref.shape[2], lane_chunk):
        lanes = slice(c0, c0 + lane_chunk)
        for r in range(rows_per_tile):
            w = jnp.zeros((nk, lane_chunk), F32)
            for h in range(PEER_HEADS):
                pair = s2_ref[h, :, lanes] + s1_ref[h, r:r + 1, lanes]
                val = e2_ref[h, :, lanes] * e1_ref[h, r:r + 1, lanes]
                w = w + jnp.where(pair >= tau_ref[h, 0:1, lanes], val, 0.0)
            w_ref[r * nk:(r + 1) * nk, lanes] = w
    act = w_ref[...] * _gelu_exact(h_t)
    acc_ref[...] += _dot_tn(act, v_ref[...])

    @pl.when(e == pl.num_programs(1) - 1)
    def _():
        y = x_ref[...] + acc_ref[...]
        if final_norm:
            y = _rms(y, lnf_ref[...])
        o_ref[...] = y


def _peer_main(x, z, s1, s2, e1, e2, tau, u_tab, v_tab, lnf_w, final_norm, tt, te):
    t, d = x.shape
    hp, nk = PEER_HEADS, PEER_NKEYS
    n_exp = u_tab.shape[0]
    stat = lambda rows: pl.BlockSpec((hp, rows, tt), lambda i, e: (0, 0, i))
    rows_per_tile = te // nk
    assert rows_per_tile == 8
    tile_rows = pl.BlockSpec((hp, rows_per_tile, tt), lambda i, e: (0, e, i))
    return pl.pallas_call(
        functools.partial(_peer_main_kernel, rows_per_tile=rows_per_tile, final_norm=final_norm),
        grid=(t // tt, n_exp // te),
        in_specs=[pl.BlockSpec((tt, d), lambda i, e: (i, 0)),
                  pl.BlockSpec((tt, d), lambda i, e: (i, 0)),
                  tile_rows, stat(nk), tile_rows, stat(nk), stat(8),
                  pl.BlockSpec((te, d), lambda i, e: (e, 0)),
                  pl.BlockSpec((te, d), lambda i, e: (e, 0)),
                  pl.BlockSpec((1, d), lambda i, e: (0, 0))],
        out_specs=pl.BlockSpec((tt, d), lambda i, e: (i, 0)),
        out_shape=jax.ShapeDtypeStruct((t, d), F32),
        scratch_shapes=[pltpu.VMEM((tt, d), F32), pltpu.VMEM((te, tt), F32)],
        compiler_params=_params("parallel", "arbitrary"),
        name="peer_dense",
    )(x, z, s1, s2, e1, e2, tau, u_tab.astype(BF16), v_tab.astype(BF16), lnf_w.reshape(1, d))


def _pad_cols(w, mult):
    pad = (-w.shape[1]) % mult
    return jnp.pad(w, ((0, 0), (0, pad))) if pad else w


def _split(z, sizes):
    out, o = [], 0
    for sz in sizes:
        out.append(z[..., o:o + sz])
        o += sz
    return out


def _layer(x, b, s, p, l, nsa_tbl, swa_tbl, lnf_w, final_norm):
    t, d = x.shape
    hd = HEAD_DIM
    proj = _rms_matmul(x, p["ln1_w"][l], _pad_cols(p["w_in"][l], 512).astype(BF16), 512, 512)
    nsa_sizes = (NSA_Q_HEADS * hd,) + (NSA_KV_GROUPS * hd,) * 6 + (NSA_Q_HEADS * 3,)
    swa_sizes = (SWA_Q_HEADS * hd, SWA_KV_HEADS * hd, SWA_KV_HEADS * hd)
    cols = _split(proj, (RWKV_COLS,) + nsa_sizes + swa_sizes + (d, d, d))
    rw = cols[0].reshape(b, s, RWKV_COLS)
    rw_prev = jnp.pad(rw, ((0, 0), (1, 0), (0, 0)))[:, :-1]
    rw = rw + p["rwkv_mu"][l] * (rw_prev - rw)
    o_a = _rwkv(rw, p["rwkv_w0"][l], p["rwkv_w2"][l], p["rwkv_a0"][l], p["rwkv_a2"][l], p["rwkv_g2"][l],
                p["rwkv_k_k"][l], p["rwkv_k_a"][l], p["rwkv_r_k"][l], p["rwkv_ln_w"][l], p["rwkv_ln_b"][l])
    seq = lambda z: z.reshape(b, s, z.shape[-1])
    nq, nkc, nvc, nks, nvs, nkw, nvw, ngate = [seq(c) for c in cols[1:9]]
    o_b = _nsa(nq, nkc, nvc, nks, nvs, nkw, nvw, ngate, p["nsa_pe_k"][l], p["nsa_pe_v"][l],
               p["nsa_ck_w1"][l], p["nsa_ck_w2"][l], p["nsa_cv_w1"][l], p["nsa_cv_w2"][l], nsa_tbl)
    sq, sk, sv = [seq(c) for c in cols[9:12]]
    o_c = _swa(sq, sk, sv, p["swa_sinks"][l], swa_tbl)
    g_a, g_b, g_c = cols[12:15]
    x = _merge(x, o_a.reshape(t, -1), o_b.reshape(t, -1), o_c.reshape(t, -1), g_a, g_b, g_c,
               p["w_br_a"][l], p["w_br_b"][l], p["w_br_c"][l], p["w_out"][l], 256)
    z, s1, s2, e1, e2, tau = _peer_scores(x, p["ln2_w"][l], p["peer_wq"][l], p["peer_subkeys"][l], 256)
    return _peer_main(x, z, s1, s2, e1, e2, tau, p["peer_u"][l], p["peer_v"][l], lnf_w, final_norm, 512, 1024)


def kernel(x, ln1_w, ln2_w, lnf_w, rel_bias, w_in, rwkv_mu, rwkv_w0, rwkv_w2, rwkv_a0, rwkv_a2, rwkv_g2, rwkv_k_k, rwkv_k_a, rwkv_r_k, rwkv_ln_w, rwkv_ln_b, nsa_pe_k, nsa_pe_v, nsa_ck_w1, nsa_ck_w2, nsa_cv_w1, nsa_cv_w2, swa_sinks, w_br_a, w_br_b, w_br_c, w_out, peer_wq, peer_subkeys, peer_u, peer_v):
    p = dict(ln1_w=ln1_w, ln2_w=ln2_w, w_in=w_in, rwkv_mu=rwkv_mu, rwkv_w0=rwkv_w0, rwkv_w2=rwkv_w2,
             rwkv_a0=rwkv_a0, rwkv_a2=rwkv_a2, rwkv_g2=rwkv_g2, rwkv_k_k=rwkv_k_k, rwkv_k_a=rwkv_k_a,
             rwkv_r_k=rwkv_r_k.reshape(rwkv_r_k.shape[0], -1), rwkv_ln_w=rwkv_ln_w, rwkv_ln_b=rwkv_ln_b,
             nsa_pe_k=nsa_pe_k, nsa_pe_v=nsa_pe_v, nsa_ck_w1=nsa_ck_w1, nsa_ck_w2=nsa_ck_w2,
             nsa_cv_w1=nsa_cv_w1, nsa_cv_w2=nsa_cv_w2, swa_sinks=swa_sinks, w_br_a=w_br_a, w_br_b=w_br_b,
             w_br_c=w_br_c, w_out=w_out, peer_wq=peer_wq, peer_subkeys=peer_subkeys, peer_u=peer_u,
             peer_v=peer_v)
    b, s, d = x.shape
    depth = w_in.shape[0]
    nsa_tbl = rel_bias[:, :NSA_Q_HEADS]
    swa_tbl = rel_bias[:, NSA_Q_HEADS:]
    y = x.reshape(b * s, d)
    for l in range(depth):
        y = _layer(y, b, s, p, l, nsa_tbl, swa_tbl, lnf_w, l == depth - 1)
    return y.reshape(b, s, d)
```

```python
import functools
import math

import jax
import jax.numpy as jnp
from jax import lax
from jax.experimental import pallas as pl
from jax.experimental.pallas import tpu as pltpu

F32 = jnp.float32
BF16 = jnp.bfloat16

D_MODEL = 1024
HEAD_DIM = 64
Q_BLOCK = 128
NORM_EPS = 1e-6

RWKV_HEADS = 8
RWKV_WIDTH = RWKV_HEADS * HEAD_DIM
RWKV_DECAY_RANK = 64
RWKV_ICLR_RANK = 64
RWKV_GATE_RANK = 128
RWKV_GN_EPS = 64e-5
RWKV_COLS = 3 * RWKV_WIDTH + RWKV_DECAY_RANK + RWKV_ICLR_RANK + RWKV_GATE_RANK
RWKV_CHUNK = 64

NSA_Q_HEADS = 8
NSA_KV_GROUPS = 2
NSA_HG = NSA_Q_HEADS // NSA_KV_GROUPS
NSA_CMP_LEN = 32
NSA_CMP_STRIDE = 16
NSA_CMP_HIDDEN = 256
NSA_SEL_BLOCK = 64
NSA_SEL_TOPN = 16
NSA_WINDOW = 512
NSA_FORCE_SCORE = 1e9

SWA_Q_HEADS = 8
SWA_KV_HEADS = 2
SWA_HG = SWA_Q_HEADS // SWA_KV_HEADS
SWA_WINDOW = 128

REL_BUCKETS = 32
REL_MAX_DIST = 128

PEER_HEADS = 8
PEER_NKEYS = 128
PEER_QDIM = 256
PEER_TOPK = 16
PEER_LANE_CHUNK = 128

NEG_BIG = -1e30
VMEM_LIMIT_BYTES = 56 * 1024 * 1024


def _params(*sem):
    return pltpu.CompilerParams(dimension_semantics=sem, vmem_limit_bytes=VMEM_LIMIT_BYTES)


def _dot(a, b):
    return jnp.dot(a.astype(BF16), b.astype(BF16), preferred_element_type=F32)


def _dot_nt(a, b):
    return lax.dot_general(a.astype(BF16), b.astype(BF16), (((1,), (1,)), ((), ())),
                           preferred_element_type=F32)


def _dot_tn(a, b):
    return lax.dot_general(a.astype(BF16), b.astype(BF16), (((0,), (0,)), ((), ())),
                           preferred_element_type=F32)


def _split_dot(a, b_exact):
    hi = a.astype(BF16)
    lo = (a - hi.astype(F32)).astype(BF16)
    return (jnp.dot(hi, b_exact, preferred_element_type=F32)
            + jnp.dot(lo, b_exact, preferred_element_type=F32))


def _split_dot_rhs(a_exact, b):
    hi = b.astype(BF16)
    lo = (b - hi.astype(F32)).astype(BF16)
    return (jnp.dot(a_exact, hi, preferred_element_type=F32)
            + jnp.dot(a_exact, lo, preferred_element_type=F32))


def _sigmoid(x):
    return 1.0 / (1.0 + jnp.exp(-x))


def _gelu_exact(x):
    return 0.5 * x * (1.0 + lax.erf(x * (1.0 / math.sqrt(2.0))))


def _rms(x, g):
    return x * lax.rsqrt(jnp.mean(x * x, axis=-1, keepdims=True) + NORM_EPS) * g


def _rms_matmul_kernel(x_ref, g_ref, w_ref, o_ref):
    y = _rms(x_ref[...], g_ref[...])
    o_ref[...] = jnp.dot(y.astype(BF16), w_ref[...], preferred_element_type=F32)


def _rms_matmul(x, g, w, tm, tn):
    t, d = x.shape
    n = w.shape[1]
    return pl.pallas_call(
        _rms_matmul_kernel,
        grid=(t // tm, n // tn),
        in_specs=[pl.BlockSpec((tm, d), lambda i, j: (i, 0)),
                  pl.BlockSpec((1, d), lambda i, j: (0, 0)),
                  pl.BlockSpec((d, tn), lambda i, j: (0, j))],
        out_specs=pl.BlockSpec((tm, tn), lambda i, j: (i, j)),
        out_shape=jax.ShapeDtypeStruct((t, n), F32),
        compiler_params=_params("parallel", "arbitrary"),
        name="rms_proj",
    )(x, g.reshape(1, d), w)


def _rwkv_kernel(z_ref, w0_ref, a0_ref, kk_ref, ka_ref, rk_ref, lnw_ref, lnb_ref,
                 w2_ref, a2_ref, g2_ref, tri_ref, bd_ref, o_ref, st_ref):
    c = RWKV_CHUNK
    n = HEAD_DIM
    w_ = RWKV_WIDTH

    @pl.when(pl.program_id(1) == 0)
    def _():
        st_ref[...] = jnp.zeros_like(st_ref)

    z = z_ref[0]
    zr = z[:, 0:w_]
    zk = z[:, w_:2 * w_]
    zv = z[:, 2 * w_:3 * w_]
    o1 = 3 * w_
    zw = z[:, o1:o1 + RWKV_DECAY_RANK]
    za = z[:, o1 + RWKV_DECAY_RANK:o1 + RWKV_DECAY_RANK + RWKV_ICLR_RANK]
    zg = z[:, o1 + RWKV_DECAY_RANK + RWKV_ICLR_RANK:]
    bd = bd_ref[...]
    tri = tri_ref[...]

    y = -(w0_ref[...] + _dot(jnp.tanh(zw), w2_ref[...]))
    softplus = jnp.maximum(y, 0.0) + jnp.log(1.0 + jnp.exp(-jnp.abs(y)))
    lw = -jnp.exp(-softplus - 0.5)
    a = _sigmoid(a0_ref[...] + _dot(za, a2_ref[...]))
    g = _dot(_sigmoid(zg), g2_ref[...])
    kk = zk * kk_ref[...]
    kk = kk * lax.rsqrt(jnp.maximum(_split_dot(kk * kk, bd), 1e-12))
    k = zk * (1.0 + (a - 1.0) * ka_ref[...])

    cum = _split_dot_rhs(tri, lw)
    e_pos = jnp.exp(cum)
    e_neg = jnp.exp(-cum)
    a_t = -kk * jnp.exp(cum - lw)
    b_t = kk * a * e_neg
    k_t = k * e_neg
    r_t = zr * e_pos

    ri = lax.broadcasted_iota(jnp.int32, (c, c), 0)
    ci = lax.broadcasted_iota(jnp.int32, (c, c), 1)
    strict = ri > ci
    incl = ri >= ci
    eye = (ri == ci).astype(F32)

    heads = range(RWKV_HEADS)
    sl = [slice(h * n, (h + 1) * n) for h in heads]
    ah = [a_t[:, s] for s in sl]
    bh = [b_t[:, s] for s in sl]
    kh = [k_t[:, s] for s in sl]
    rh = [r_t[:, s] for s in sl]
    vh = [zv[:, s] for s in sl]
    s0 = [st_ref[h] for h in heads]
    p = [jnp.where(strict, _dot_nt(ah[h], bh[h]), 0.0) for h in heads]
    l_ak = [jnp.where(strict, _dot_nt(ah[h], kh[h]), 0.0) for h in heads]
    m_rb = [jnp.where(incl, _dot_nt(rh[h], bh[h]), 0.0) for h in heads]
    m_rk = [jnp.where(incl, _dot_nt(rh[h], kh[h]), 0.0) for h in heads]
    rhs = [_dot_nt(ah[h], s0[h]) + _dot(l_ak[h], vh[h]) for h in heads]
    y0 = [_dot_nt(rh[h], s0[h]) + _dot(m_rk[h], vh[h]) for h in heads]
    s1 = [s0[h] + _dot_tn(vh[h], kh[h]) for h in heads]
    tinv = [eye + p[h] for h in heads]
    for _ in range(int(math.log2(c)) - 1):
        p = [_dot(p[h], p[h]) for h in heads]
        tinv = [tinv[h] + _dot(p[h], tinv[h]) for h in heads]
    u = [_dot(tinv[h], rhs[h]) for h in heads]
    ys = [y0[h] + _dot(m_rb[h], u[h]) for h in heads]
    for h in heads:
        st_ref[h] = (s1[h] + _dot_tn(u[h], bh[h])) * e_pos[c - 1:c, sl[h]]
    yv = jnp.concatenate(ys, axis=-1)

    mu = _split_dot(yv, bd) * (1.0 / n)
    dlt = yv - mu
    var = _split_dot(dlt * dlt, bd) * (1.0 / n)
    yn = dlt * lax.rsqrt(var + RWKV_GN_EPS) * lnw_ref[...] + lnb_ref[...]
    bonus = _split_dot(zr * k * rk_ref[...], bd) * zv
    o_ref[0] = (yn + bonus) * g


def _rwkv(rw, w0, w2, a0, a2, g2, k_k, k_a, r_k, ln_w, ln_b):
    b, s, cols = rw.shape
    c = RWKV_CHUNK
    w_ = RWKV_WIDTH
    row = lambda v: v.reshape(1, w_).astype(F32)
    idx = jnp.arange(w_) // HEAD_DIM
    bd = (idx[:, None] == idx[None, :]).astype(BF16)
    tri = (jnp.arange(c)[:, None] >= jnp.arange(c)[None, :]).astype(BF16)
    full = lambda shp: pl.BlockSpec(shp, lambda i, j: (0,) * len(shp))
    return pl.pallas_call(
        _rwkv_kernel,
        grid=(b, s // c),
        in_specs=[pl.BlockSpec((1, c, cols), lambda i, j: (i, j, 0))]
                 + [full((1, w_))] * 7
                 + [full((RWKV_DECAY_RANK, w_)), full((RWKV_ICLR_RANK, w_)), full((RWKV_GATE_RANK, w_)),
                    full((c, c)), full((w_, w_))],
        out_specs=pl.BlockSpec((1, c, w_), lambda i, j: (i, j, 0)),
        out_shape=jax.ShapeDtypeStruct((b, s, w_), F32),
        scratch_shapes=[pltpu.VMEM((RWKV_HEADS, HEAD_DIM, HEAD_DIM), F32)],
        compiler_params=_params("parallel", "arbitrary"),
        name="rwkv7_chunked",
    )(rw, row(w0), row(a0), row(k_k), row(k_a), row(r_k), row(ln_w), row(ln_b),
      w2.astype(BF16), a2.astype(BF16), g2.astype(BF16), tri, bd)


def _cmp_kernel(f_ref, pe_ref, w1_ref, w2_ref, o_ref):
    f = f_ref[0, 0] + pe_ref[0]
    h = _gelu_exact(_dot(f, w1_ref[0]))
    o_ref[0, 0] = _dot(h, w2_ref[0])


def _nsa_compress(flat, pe, w1, w2):
    two, bg, ncp, lin = flat.shape
    hid = w1.shape[-1]
    return pl.pallas_call(
        _cmp_kernel,
        grid=(two, bg),
        in_specs=[pl.BlockSpec((1, 1, ncp, lin), lambda i, j: (i, j, 0, 0)),
                  pl.BlockSpec((1, 1, lin), lambda i, j: (i, 0, 0)),
                  pl.BlockSpec((1, lin, hid), lambda i, j: (i, 0, 0)),
                  pl.BlockSpec((1, hid, HEAD_DIM), lambda i, j: (i, 0, 0))],
        out_specs=pl.BlockSpec((1, 1, ncp, HEAD_DIM), lambda i, j: (i, j, 0, 0)),
        out_shape=jax.ShapeDtypeStruct((two, bg, ncp, HEAD_DIM), F32),
        compiler_params=_params("parallel", "parallel"),
        name="nsa_compress",
    )(flat, pe, w1.astype(BF16), w2.astype(BF16))


def _attn_step(k_tile, vt_tile, qt, bias, m_ref, l_ref, acc_ref):
    s = jnp.dot(k_tile, qt, preferred_element_type=F32) + bias
    m_old = m_ref[0:1, :]
    m_new = jnp.maximum(m_old, jnp.max(s, axis=0, keepdims=True))
    p = jnp.exp(s - m_new)
    alpha = jnp.exp(m_old - m_new)
    l_ref[0:1, :] = alpha * l_ref[0:1, :] + jnp.sum(p, axis=0, keepdims=True)
    acc_ref[...] = alpha * acc_ref[...] + jnp.dot(vt_tile, p.astype(BF16), preferred_element_type=F32)
    m_ref[0:1, :] = m_new


def _key_rows(ref, j):
    return ref[0, 0, pl.ds(pl.multiple_of(j * Q_BLOCK, Q_BLOCK), Q_BLOCK), :]


def _nsa_kernel(qt_ref, gate_ref, kc_ref, vct_ref, ks_ref, vst_ref, kw_ref, vwt_ref, ovlt_ref,
                d0_ref, d1_ref, edge_ref, far_ref, o_ref, sb_ref, m_ref, l_ref, acc_ref,
                *, n_sel, n_cmp_pad):
    qb = Q_BLOCK
    hg = NSA_HG
    i = pl.program_id(2)
    s0 = i * qb
    qt = qt_ref[0, 0, 0]

    crow = lax.broadcasted_iota(jnp.int32, (n_cmp_pad, qb), 0)
    tcol = s0 + lax.broadcasted_iota(jnp.int32, (n_cmp_pad, qb), 1)
    cmask = (NSA_CMP_STRIDE * crow + NSA_CMP_LEN - 1) <= tcol
    st = jnp.dot(kc_ref[0, 0], qt, preferred_element_type=F32)
    vct = vct_ref[0, 0]
    psum = jnp.zeros((n_cmp_pad, qb), F32)
    o_cmp = []
    for h in range(hg):
        s = jnp.where(cmask, st[:, h * qb:(h + 1) * qb], NEG_BIG)
        mx = jnp.max(s, axis=0, keepdims=True)
        e = jnp.where(cmask, jnp.exp(s - mx), 0.0)
        p = e / jnp.maximum(jnp.sum(e, axis=0, keepdims=True), 1e-30)
        o_cmp.append(jnp.dot(vct, p.astype(BF16), preferred_element_type=F32))
        psum = psum + p

    imp = _split_dot_rhs(ovlt_ref[...], psum)
    nrow = lax.broadcasted_iota(jnp.int32, (n_sel, qb), 0)
    cur = lax.shift_right_logical(s0 + lax.broadcasted_iota(jnp.int32, (n_sel, qb), 1),
                                  int(math.log2(NSA_SEL_BLOCK)))
    forced = (nrow == 0) | (nrow == cur) | (nrow == cur - 1)
    score = jnp.where(forced, NSA_FORCE_SCORE, jnp.where(nrow <= cur, imp, -1.0))
    sb = jnp.full((n_sel, qb), NEG_BIG, F32)
    for _ in range(min(NSA_SEL_TOPN, n_sel)):
        mx = jnp.max(score, axis=0, keepdims=True)
        first = jnp.min(jnp.where(score == mx, nrow, n_sel), axis=0, keepdims=True)
        hit = nrow == first
        sb = jnp.where(hit, 0.0, sb)
        score = jnp.where(hit, -jnp.inf, score)
    sb_ref[...] = sb

    blocks_per_tile = qb // NSA_SEL_BLOCK

    def sel_bias(j0, ntiles):
        rows = [jnp.broadcast_to(sb_ref[pl.ds(blocks_per_tile * j0 + r, 1), :], (NSA_SEL_BLOCK, qb))
                for r in range(blocks_per_tile * ntiles)]
        return jnp.concatenate([jnp.concatenate(rows, axis=0)] * hg, axis=1)

    def reset():
        m_ref[...] = jnp.full_like(m_ref, NEG_BIG)
        l_ref[...] = jnp.zeros_like(l_ref)
        acc_ref[...] = jnp.zeros_like(acc_ref)

    def finish():
        return acc_ref[...] / jnp.maximum(l_ref[0:1, :], 1e-30)

    def step(k_ref, vt_ref, j0, ntiles, bias):
        k_rows = k_ref[0, 0, pl.ds(pl.multiple_of(j0 * qb, qb), ntiles * qb), :]
        vt = jnp.concatenate([vt_ref[0, 0, j0 + t] for t in range(ntiles)], axis=1)
        _attn_step(k_rows, vt, qt, bias, m_ref, l_ref, acc_ref)

    def near_step(k_ref, vt_ref, selected):
        @pl.when(i >= 1)
        def _():
            bias = jnp.concatenate([d1_ref[0], d0_ref[0]], axis=0)
            step(k_ref, vt_ref, i - 1, 2, bias + sel_bias(i - 1, 2) if selected else bias)

        @pl.when(i == 0)
        def _():
            bias = d0_ref[0]
            step(k_ref, vt_ref, 0, 1, bias + sel_bias(0, 1) if selected else bias)

    far = far_ref[0, 0:1, :]
    far_group = 4

    reset()
    near_step(ks_ref, vst_ref, True)
    n_far = jnp.maximum(i - 1, 0)
    n_grp = lax.shift_right_logical(n_far, int(math.log2(far_group)))

    def far_group_body(gi, carry):
        step(ks_ref, vst_ref, gi * far_group, far_group, sel_bias(gi * far_group, far_group) + far)
        return carry

    def far_tile_body(j, carry):
        step(ks_ref, vst_ref, j, 1, sel_bias(j, 1) + far)
        return carry

    lax.fori_loop(0, n_grp, far_group_body, 0)
    lax.fori_loop(n_grp * far_group, n_far, far_tile_body, 0)
    o_sel = finish()

    reset()
    near_step(kw_ref, vwt_ref, False)
    n_back = NSA_WINDOW // qb
    n_mid = n_back - 2

    @pl.when(i >= n_back - 1)
    def _():
        step(kw_ref, vwt_ref, i - (n_back - 1), n_mid, far)

    for d in range(2, n_back):
        @pl.when((i >= d) & (i < n_back - 1))
        def _(d=d):
            step(kw_ref, vwt_ref, i - d, 1, far)

    @pl.when(i >= n_back)
    def _():
        step(kw_ref, vwt_ref, i - n_back, 1, edge_ref[...] + far)
    o_win = finish()

    gs = _sigmoid(gate_ref[0, 0, 0])
    o_ref[0, 0, 0] = (gs[0:1, :] * jnp.concatenate(o_cmp, axis=1) + gs[1:2, :] * o_sel + gs[2:3, :] * o_win)


def _t5_bucket(dist):
    n = jnp.maximum(dist, 0)
    max_exact = REL_BUCKETS // 2
    nf = jnp.maximum(n, max_exact).astype(F32)
    large = max_exact + (jnp.log(nf / max_exact) / math.log(REL_MAX_DIST / max_exact)
                         * (REL_BUCKETS - max_exact)).astype(jnp.int32)
    large = jnp.minimum(large, REL_BUCKETS - 1)
    return jnp.where(n < max_exact, n, large)


def _bias_tiles(tbl, groups, hg):
    qb = Q_BLOCK
    kj = jnp.arange(qb)[:, None]
    qi = jnp.arange(qb)[None, :]
    t = tbl.T.reshape(groups, hg, REL_BUCKETS).astype(F32)
    lay = lambda x: jnp.transpose(x, (0, 2, 1, 3)).reshape(groups, qb, hg * qb)
    d0 = lay(t[:, :, _t5_bucket(qi - kj)] + jnp.where(kj <= qi, 0.0, NEG_BIG))
    d1 = lay(t[:, :, _t5_bucket(qi - kj + qb)])
    edge = jnp.tile(jnp.where(qi < kj, 0.0, NEG_BIG).astype(F32), (1, hg))
    far = jnp.broadcast_to(t[:, :, REL_BUCKETS - 1][:, None, :, None], (groups, 8, hg, qb))
    return d0, d1, edge, far.reshape(groups, 8, hg * qb)


def _tiles_t(z, b, s, g):
    nt = s // Q_BLOCK
    z = z.reshape(b, nt, Q_BLOCK, g, HEAD_DIM)
    return jnp.transpose(z, (0, 3, 1, 4, 2)).astype(BF16)


def _rows(z, b, s, g):
    return jnp.transpose(z.reshape(b, s, g, HEAD_DIM), (0, 2, 1, 3)).astype(BF16)


def _q_t(q, b, s, g, hg):
    nq = s // Q_BLOCK
    q = q.reshape(b, nq, Q_BLOCK, g, hg, HEAD_DIM) * (HEAD_DIM ** -0.5)
    return jnp.transpose(q, (0, 3, 1, 5, 4, 2)).reshape(b, g, nq, HEAD_DIM, hg * Q_BLOCK).astype(BF16)


def _o_from_t(o_t, b, s, g, hg):
    nq = s // Q_BLOCK
    o = o_t.reshape(b, g, nq, HEAD_DIM, hg, Q_BLOCK)
    return jnp.transpose(o, (0, 2, 5, 1, 4, 3)).reshape(b, s, g * hg * HEAD_DIM)


def _nsa(q, kc, vc, ks, vs, kw, vw, gates, pe_k, pe_v, ck_w1, ck_w2, cv_w1, cv_w2, tbl):
    b, s, _ = q.shape
    g, hg, dh = NSA_KV_GROUPS, NSA_HG, HEAD_DIM
    qb = Q_BLOCK
    nchunk = s // NSA_CMP_STRIDE
    n_sub = NSA_CMP_LEN // NSA_CMP_STRIDE
    n_cmp = nchunk - n_sub + 1
    n_sel = s // NSA_SEL_BLOCK
    nt = s // qb

    def flat_blocks(z):
        ch = z.reshape(b, nchunk, NSA_CMP_STRIDE, g, dh)
        ch = jnp.pad(ch, ((0, 0), (0, n_sub - 1), (0, 0), (0, 0), (0, 0)))
        blk = jnp.concatenate([ch[:, j:j + nchunk] for j in range(n_sub)], axis=2)
        return jnp.transpose(blk, (0, 3, 1, 2, 4)).reshape(b * g, nchunk, NSA_CMP_LEN * dh)

    flat = jnp.stack([flat_blocks(kc), flat_blocks(vc)])
    pe = jnp.stack([pe_k.reshape(1, -1), pe_v.reshape(1, -1)])
    cmp_kv = _nsa_compress(flat, pe, jnp.stack([ck_w1, cv_w1]), jnp.stack([ck_w2, cv_w2]))
    cmp_kv = cmp_kv.reshape(2, b, g, nchunk, dh)
    kcm = cmp_kv[0].astype(BF16)
    vct = jnp.transpose(cmp_kv[1], (0, 1, 3, 2)).astype(BF16)

    cmp_start = jnp.arange(nchunk) * NSA_CMP_STRIDE
    cmp_end = cmp_start + NSA_CMP_LEN - 1
    sel_start = jnp.arange(n_sel) * NSA_SEL_BLOCK
    ovlt = ((cmp_end[None, :] >= sel_start[:, None])
            & (cmp_start[None, :] <= sel_start[:, None] + NSA_SEL_BLOCK - 1)
            & (jnp.arange(nchunk)[None, :] < n_cmp)).astype(BF16)

    d0, d1, edge, far = _bias_tiles(tbl, g, hg)
    gt = gates.reshape(b, nt, qb, g, hg, 3)
    gt = jnp.transpose(gt, (0, 3, 1, 5, 4, 2)).reshape(b, g, nt, 3, hg * qb)
    gt = jnp.pad(gt, ((0, 0), (0, 0), (0, 0), (0, 5), (0, 0)))

    wide = hg * qb
    per_q = lambda rows: pl.BlockSpec((1, 1, 1, rows, wide), lambda bi, gi, i: (bi, gi, i, 0, 0))
    rows_spec = pl.BlockSpec((1, 1, s, dh), lambda bi, gi, i: (bi, gi, 0, 0))
    tiles_spec = pl.BlockSpec((1, 1, nt, dh, qb), lambda bi, gi, i: (bi, gi, 0, 0, 0))
    bias_spec = pl.BlockSpec((1, qb, wide), lambda bi, gi, i: (gi, 0, 0))
    o_t = pl.pallas_call(
        functools.partial(_nsa_kernel, n_sel=n_sel, n_cmp_pad=nchunk),
        grid=(b, g, nt),
        in_specs=[per_q(dh), per_q(8),
                  pl.BlockSpec((1, 1, nchunk, dh), lambda bi, gi, i: (bi, gi, 0, 0)),
                  pl.BlockSpec((1, 1, dh, nchunk), lambda bi, gi, i: (bi, gi, 0, 0)),
                  rows_spec, tiles_spec, rows_spec, tiles_spec,
                  pl.BlockSpec((n_sel, nchunk), lambda bi, gi, i: (0, 0)),
                  bias_spec, bias_spec,
                  pl.BlockSpec((qb, wide), lambda bi, gi, i: (0, 0)),
                  pl.BlockSpec((1, 8, wide), lambda bi, gi, i: (gi, 0, 0))],
        out_specs=per_q(dh),
        out_shape=jax.ShapeDtypeStruct((b, g, nt, dh, wide), F32),
        scratch_shapes=[pltpu.VMEM((n_sel, qb), F32), pltpu.VMEM((8, wide), F32),
                        pltpu.VMEM((8, wide), F32), pltpu.VMEM((dh, wide), F32)],
        compiler_params=_params("parallel", "parallel", "arbitrary"),
        name="nsa_attention",
    )(_q_t(q, b, s, g, hg), gt, kcm, vct,
      _rows(ks, b, s, g), _tiles_t(vs, b, s, g), _rows(kw, b, s, g), _tiles_t(vw, b, s, g),
      ovlt, d0, d1, edge, far)
    return _o_from_t(o_t, b, s, g, hg)


def _swa_kernel(qt_ref, k_ref, vt_ref, d0_ref, d1_ref, sink_ref, o_ref, m_ref, l_ref, acc_ref):
    i = pl.program_id(2)
    qt = qt_ref[0, 0, 0]
    m_ref[...] = sink_ref[0]
    l_ref[...] = jnp.ones_like(l_ref)
    acc_ref[...] = jnp.zeros_like(acc_ref)
    _attn_step(_key_rows(k_ref, i), vt_ref[0, 0, i], qt, d0_ref[0], m_ref, l_ref, acc_ref)

    @pl.when(i >= 1)
    def _():
        _attn_step(_key_rows(k_ref, i - 1), vt_ref[0, 0, i - 1], qt, d1_ref[0], m_ref, l_ref, acc_ref)

    o_ref[0, 0, 0] = acc_ref[...] / l_ref[0:1, :]


def _swa(q, k, v, sinks, tbl):
    b, s, _ = q.shape
    g, hg, dh = SWA_KV_HEADS, SWA_HG, HEAD_DIM
    qb = Q_BLOCK
    nt = s // qb
    assert SWA_WINDOW == qb
    d0, d1, edge, _ = _bias_tiles(tbl, g, hg)
    d1 = d1 + edge[None]
    wide = hg * qb
    sink = jnp.broadcast_to(sinks.reshape(g, 1, hg, 1).astype(F32), (g, 8, hg, qb)).reshape(g, 8, wide)
    per_q = pl.BlockSpec((1, 1, 1, dh, wide), lambda bi, gi, i: (bi, gi, i, 0, 0))
    bias_spec = pl.BlockSpec((1, qb, wide), lambda bi, gi, i: (gi, 0, 0))
    o_t = pl.pallas_call(
        _swa_kernel,
        grid=(b, g, nt),
        in_specs=[per_q,
                  pl.BlockSpec((1, 1, s, dh), lambda bi, gi, i: (bi, gi, 0, 0)),
                  pl.BlockSpec((1, 1, nt, dh, qb), lambda bi, gi, i: (bi, gi, 0, 0, 0)),
                  bias_spec, bias_spec,
                  pl.BlockSpec((1, 8, wide), lambda bi, gi, i: (gi, 0, 0))],
        out_specs=per_q,
        out_shape=jax.ShapeDtypeStruct((b, g, nt, dh, wide), F32),
        scratch_shapes=[pltpu.VMEM((8, wide), F32), pltpu.VMEM((8, wide), F32), pltpu.VMEM((dh, wide), F32)],
        compiler_params=_params("parallel", "parallel", "arbitrary"),
        name="swa_sink_attention",
    )(_q_t(q, b, s, g, hg), _rows(k, b, s, g), _tiles_t(v, b, s, g), d0, d1, sink)
    return _o_from_t(o_t, b, s, g, hg)


def _merge_kernel(x_ref, oa_ref, ob_ref, oc_ref, ga_ref, gb_ref, gc_ref,
                  wa_ref, wb_ref, wc_ref, wo_ref, o_ref):
    merged = (_sigmoid(ga_ref[...]) * _dot(oa_ref[...], wa_ref[...])
              + _sigmoid(gb_ref[...]) * _dot(ob_ref[...], wb_ref[...])
              + _sigmoid(gc_ref[...]) * _dot(oc_ref[...], wc_ref[...]))
    o_ref[...] = x_ref[...] + _dot(merged, wo_ref[...])


def _merge(x, o_a, o_b, o_c, g_a, g_b, g_c, w_a, w_b, w_c, w_o, tm):
    t, d = x.shape
    tok = lambda w: pl.BlockSpec((tm, w), lambda i: (i, 0))
    full = lambda a: pl.BlockSpec(a.shape, lambda i: (0, 0))
    ws = [w.astype(BF16) for w in (w_a, w_b, w_c, w_o)]
    return pl.pallas_call(
        _merge_kernel,
        grid=(t // tm,),
        in_specs=[tok(d), tok(o_a.shape[1]), tok(o_b.shape[1]), tok(o_c.shape[1]), tok(d), tok(d), tok(d)]
                 + [full(w) for w in ws],
        out_specs=tok(d),
        out_shape=jax.ShapeDtypeStruct((t, d), F32),
        compiler_params=_params("parallel"),
        name="branch_merge",
    )(x, o_a, o_b, o_c, g_a, g_b, g_c, *ws)


def _top_rows(x, k):
    r = x.shape[0]
    ridx = lax.broadcasted_iota(jnp.int32, x.shape, 0)
    vals = []
    for _ in range(k):
        mx = jnp.max(x, axis=0, keepdims=True)
        first = jnp.min(jnp.where(x == mx, ridx, r), axis=0, keepdims=True)
        x = jnp.where(ridx == first, -jnp.inf, x)
        vals.append(mx)
    return jnp.concatenate(vals, axis=0)


def _peer_score_kernel(x_ref, g_ref, wqt_ref, sk_ref, z_ref, s1_ref, s2_ref, e1_ref, e2_ref, tau_ref):
    kk = PEER_TOPK
    half = PEER_QDIM // 2
    tt = x_ref.shape[0]
    lane_chunk = PEER_LANE_CHUNK
    zb = _rms(x_ref[...], g_ref[...]).astype(BF16)
    z_ref[...] = zb
    q_t = lax.dot_general(wqt_ref[...], zb, (((1,), (1,)), ((), ())), preferred_element_type=F32)
    q_t = q_t.astype(BF16)
    for h in range(PEER_HEADS):
        s = [jnp.dot(sk_ref[h, p], q_t[(2 * h + p) * half:(2 * h + p + 1) * half, :],
                     preferred_element_type=F32) for p in range(2)]
        top = [_top_rows(s[p], kk) for p in range(2)]
        pieces = [top[0][i:i + 1] + top[1][0:kk // (i + 1)] for i in range(kk)]
        n_cand = sum(pc.shape[0] for pc in pieces)
        pieces.append(jnp.full(((-n_cand) % 8, tt), -jnp.inf, F32))
        best = _top_rows(jnp.concatenate(pieces, axis=0), kk)
        zsum = jnp.sum(jnp.exp(best - best[0:1]), axis=0, keepdims=True)
        e1 = jnp.exp(s[0] - top[0][0:1]) / zsum
        e2 = jnp.exp(s[1] - top[1][0:1])
        tau = jnp.broadcast_to(best[kk - 1:kk], (8, tt))
        for c in range(tt // lane_chunk):
            lanes = slice(c * lane_chunk, (c + 1) * lane_chunk)
            s1_ref[h, c] = s[0][:, lanes]
            s2_ref[h, c] = s[1][:, lanes]
            e1_ref[h, c] = e1[:, lanes]
            e2_ref[h, c] = e2[:, lanes]
            tau_ref[h, c] = tau[:, lanes]


def _peer_scores(x, g, wq, subkeys, tt):
    t, d = x.shape
    hp, nk = PEER_HEADS, PEER_NKEYS
    wqt = jnp.transpose(wq).astype(BF16)
    sk = subkeys.astype(BF16)
    lc = PEER_LANE_CHUNK
    stat = lambda rows: pl.BlockSpec((hp, tt // lc, rows, lc), lambda i: (0, i, 0, 0))
    shp = lambda rows: jax.ShapeDtypeStruct((hp, t // lc, rows, lc), F32)
    return pl.pallas_call(
        _peer_score_kernel,
        grid=(t // tt,),
        in_specs=[pl.BlockSpec((tt, d), lambda i: (i, 0)),
                  pl.BlockSpec((1, d), lambda i: (0, 0)),
                  pl.BlockSpec(wqt.shape, lambda i: (0, 0)),
                  pl.BlockSpec(sk.shape, lambda i: (0, 0, 0, 0))],
        out_specs=[pl.BlockSpec((tt, d), lambda i: (i, 0)), stat(nk), stat(nk), stat(nk), stat(nk), stat(8)],
        out_shape=[jax.ShapeDtypeStruct((t, d), BF16), shp(nk), shp(nk), shp(nk), shp(nk), shp(8)],
        compiler_params=_params("parallel"),
        name="peer_scores",
    )(x, g.reshape(1, d), wqt, sk)


def _peer_main_kernel(x_ref, z_ref, s1_ref, s2_ref, e1_ref, e2_ref, tau_ref, u_ref, v_ref, lnf_ref,
                      o_ref, acc_ref, w_ref, *, rows_per_tile, final_norm):
    e = pl.program_id(1)

    @pl.when(e == 0)
    def _():
        acc_ref[...] = jnp.zeros_like(acc_ref)

    h_t = _dot_nt(u_ref[...], z_ref[...])
    nk = PEER_NKEYS
    n_chunks = s2_ref.shape[1]
    row_shift = int(math.log2(rows_per_tile))

    def weight_block(it, carry):
        c = lax.shift_right_logical(it, row_shift)
        r = jnp.bitwise_and(it, rows_per_tile - 1)
        w = jnp.zeros((nk, PEER_LANE_CHUNK), F32)
        for h in range(PEER_HEADS):
            pair = s2_ref[h, c] + s1_ref[h, c, pl.ds(r, 1), :]
            val = e2_ref[h, c] * e1_ref[h, c, pl.ds(r, 1), :]
            w = w + jnp.where(pair >= tau_ref[h, c, 0:1, :], val, 0.0)
        w_ref[c, pl.ds(pl.multiple_of(r * nk, nk), nk), :] = w
        return carry

    lax.fori_loop(0, n_chunks * rows_per_tile, weight_block, 0)
    act = jnp.concatenate([w_ref[c] for c in range(n_chunks)], axis=1) * _gelu_exact(h_t)
    acc_ref[...] += _dot_tn(act, v_ref[...])

    @pl.when(e == pl.num_programs(1) - 1)
    def _():
        y = x_ref[...] + acc_ref[...]
        if final_norm:
            y = _rms(y, lnf_ref[...])
        o_ref[...] = y


def _peer_main(x, z, s1, s2, e1, e2, tau, u_tab, v_tab, lnf_w, final_norm, tt, te):
    t, d = x.shape
    hp, nk = PEER_HEADS, PEER_NKEYS
    n_exp = u_tab.shape[0]
    lc = PEER_LANE_CHUNK
    stat = lambda rows: pl.BlockSpec((hp, tt // lc, rows, lc), lambda i, e: (0, i, 0, 0))
    rows_per_tile = te // nk
    assert rows_per_tile == 8
    tile_rows = pl.BlockSpec((hp, tt // lc, rows_per_tile, lc), lambda i, e: (0, i, e, 0))
    return pl.pallas_call(
        functools.partial(_peer_main_kernel, rows_per_tile=rows_per_tile, final_norm=final_norm),
        grid=(t // tt, n_exp // te),
        in_specs=[pl.BlockSpec((tt, d), lambda i, e: (i, 0)),
                  pl.BlockSpec((tt, d), lambda i, e: (i, 0)),
                  tile_rows, stat(nk), tile_rows, stat(nk), stat(8),
                  pl.BlockSpec((te, d), lambda i, e: (e, 0)),
                  pl.BlockSpec((te, d), lambda i, e: (e, 0)),
                  pl.BlockSpec((1, d), lambda i, e: (0, 0))],
        out_specs=pl.BlockSpec((tt, d), lambda i, e: (i, 0)),
        out_shape=jax.ShapeDtypeStruct((t, d), F32),
        scratch_shapes=[pltpu.VMEM((tt, d), F32), pltpu.VMEM((tt // lc, te, lc), F32)],
        compiler_params=_params("parallel", "arbitrary"),
        name="peer_dense",
    )(x, z, s1, s2, e1, e2, tau, u_tab.astype(BF16), v_tab.astype(BF16), lnf_w.reshape(1, d))


def _pad_cols(w, mult):
    pad = (-w.shape[1]) % mult
    return jnp.pad(w, ((0, 0), (0, pad))) if pad else w


def _split(z, sizes):
    out, o = [], 0
    for sz in sizes:
        out.append(z[..., o:o + sz])
        o += sz
    return out


def _layer(x, b, s, p, l, nsa_tbl, swa_tbl, lnf_w, final_norm):
    t, d = x.shape
    hd = HEAD_DIM
    proj = _rms_matmul(x, p["ln1_w"][l], _pad_cols(p["w_in"][l], 512).astype(BF16), 512, 512)
    nsa_sizes = (NSA_Q_HEADS * hd,) + (NSA_KV_GROUPS * hd,) * 6 + (NSA_Q_HEADS * 3,)
    swa_sizes = (SWA_Q_HEADS * hd, SWA_KV_HEADS * hd, SWA_KV_HEADS * hd)
    cols = _split(proj, (RWKV_COLS,) + nsa_sizes + swa_sizes + (d, d, d))
    rw = cols[0].reshape(b, s, RWKV_COLS)
    rw_prev = jnp.pad(rw, ((0, 0), (1, 0), (0, 0)))[:, :-1]
    rw = rw + p["rwkv_mu"][l] * (rw_prev - rw)
    o_a = _rwkv(rw, p["rwkv_w0"][l], p["rwkv_w2"][l], p["rwkv_a0"][l], p["rwkv_a2"][l], p["rwkv_g2"][l],
                p["rwkv_k_k"][l], p["rwkv_k_a"][l], p["rwkv_r_k"][l], p["rwkv_ln_w"][l], p["rwkv_ln_b"][l])
    seq = lambda z: z.reshape(b, s, z.shape[-1])
    nq, nkc, nvc, nks, nvs, nkw, nvw, ngate = [seq(c) for c in cols[1:9]]
    o_b = _nsa(nq, nkc, nvc, nks, nvs, nkw, nvw, ngate, p["nsa_pe_k"][l], p["nsa_pe_v"][l],
               p["nsa_ck_w1"][l], p["nsa_ck_w2"][l], p["nsa_cv_w1"][l], p["nsa_cv_w2"][l], nsa_tbl)
    sq, sk, sv = [seq(c) for c in cols[9:12]]
    o_c = _swa(sq, sk, sv, p["swa_sinks"][l], swa_tbl)
    g_a, g_b, g_c = cols[12:15]
    x = _merge(x, o_a.reshape(t, -1), o_b.reshape(t, -1), o_c.reshape(t, -1), g_a, g_b, g_c,
               p["w_br_a"][l], p["w_br_b"][l], p["w_br_c"][l], p["w_out"][l], 256)
    z, s1, s2, e1, e2, tau = _peer_scores(x, p["ln2_w"][l], p["peer_wq"][l], p["peer_subkeys"][l], 256)
    return _peer_main(x, z, s1, s2, e1, e2, tau, p["peer_u"][l], p["peer_v"][l], lnf_w, final_norm, 512, 1024)


def kernel(x, ln1_w, ln2_w, lnf_w, rel_bias, w_in, rwkv_mu, rwkv_w0, rwkv_w2, rwkv_a0, rwkv_a2, rwkv_g2, rwkv_k_k, rwkv_k_a, rwkv_r_k, rwkv_ln_w, rwkv_ln_b, nsa_pe_k, nsa_pe_v, nsa_ck_w1, nsa_ck_w2, nsa_cv_w1, nsa_cv_w2, swa_sinks, w_br_a, w_br_b, w_br_c, w_out, peer_wq, peer_subkeys, peer_u, peer_v):
    p = dict(ln1_w=ln1_w, ln2_w=ln2_w, w_in=w_in, rwkv_mu=rwkv_mu, rwkv_w0=rwkv_w0, rwkv_w2=rwkv_w2,
             rwkv_a0=rwkv_a0, rwkv_a2=rwkv_a2, rwkv_g2=rwkv_g2, rwkv_k_k=rwkv_k_k, rwkv_k_a=rwkv_k_a,
             rwkv_r_k=rwkv_r_k.reshape(rwkv_r_k.shape[0], -1), rwkv_ln_w=rwkv_ln_w, rwkv_ln_b=rwkv_ln_b,
             nsa_pe_k=nsa_pe_k, nsa_pe_v=nsa_pe_v, nsa_ck_w1=nsa_ck_w1, nsa_ck_w2=nsa_ck_w2,
             nsa_cv_w1=nsa_cv_w1, nsa_cv_w2=nsa_cv_w2, swa_sinks=swa_sinks, w_br_a=w_br_a, w_br_b=w_br_b,
             w_br_c=w_br_c, w_out=w_out, peer_wq=peer_wq, peer_subkeys=peer_subkeys, peer_u=peer_u,
             peer_v=peer_v)
    b, s, d = x.shape
    depth = w_in.shape[0]
    nsa_tbl = rel_bias[:, :NSA_Q_HEADS]
    swa_tbl = rel_bias[:, NSA_Q_HEADS:]
    y = x.reshape(b * s, d)
    for l in range(depth):
        y = _layer(y, b, s, p, l, nsa_tbl, swa_tbl, lnf_w, l == depth - 1)
    return y.reshape(b, s, d)
```

```python
import functools
import math

import jax
import jax.numpy as jnp
from jax import lax
from jax.experimental import pallas as pl
from jax.experimental.pallas import tpu as pltpu

F32 = jnp.float32
BF16 = jnp.bfloat16

D_MODEL = 1024
HEAD_DIM = 64
Q_BLOCK = 128
NORM_EPS = 1e-6

RWKV_HEADS = 8
RWKV_WIDTH = RWKV_HEADS * HEAD_DIM
RWKV_DECAY_RANK = 64
RWKV_ICLR_RANK = 64
RWKV_GATE_RANK = 128
RWKV_GN_EPS = 64e-5
RWKV_COLS = 3 * RWKV_WIDTH + RWKV_DECAY_RANK + RWKV_ICLR_RANK + RWKV_GATE_RANK
RWKV_CHUNK = 64

NSA_Q_HEADS = 8
NSA_KV_GROUPS = 2
NSA_HG = NSA_Q_HEADS // NSA_KV_GROUPS
NSA_CMP_LEN = 32
NSA_CMP_STRIDE = 16
NSA_CMP_HIDDEN = 256
NSA_SEL_BLOCK = 64
NSA_SEL_TOPN = 16
NSA_WINDOW = 512
NSA_FORCE_SCORE = 1e9

SWA_Q_HEADS = 8
SWA_KV_HEADS = 2
SWA_HG = SWA_Q_HEADS // SWA_KV_HEADS
SWA_WINDOW = 128

REL_BUCKETS = 32
REL_MAX_DIST = 128

PEER_HEADS = 8
PEER_NKEYS = 128
PEER_QDIM = 256
PEER_TOPK = 16
PEER_LANE_CHUNK = 128

NEG_BIG = -1e30
VMEM_LIMIT_BYTES = 56 * 1024 * 1024


def _params(*sem):
    return pltpu.CompilerParams(dimension_semantics=sem, vmem_limit_bytes=VMEM_LIMIT_BYTES)


def _dot(a, b):
    return jnp.dot(a.astype(BF16), b.astype(BF16), preferred_element_type=F32)


def _dot_nt(a, b):
    return lax.dot_general(a.astype(BF16), b.astype(BF16), (((1,), (1,)), ((), ())),
                           preferred_element_type=F32)


def _dot_tn(a, b):
    return lax.dot_general(a.astype(BF16), b.astype(BF16), (((0,), (0,)), ((), ())),
                           preferred_element_type=F32)


def _split_dot(a, b_exact):
    hi = a.astype(BF16)
    lo = (a - hi.astype(F32)).astype(BF16)
    return (jnp.dot(hi, b_exact, preferred_element_type=F32)
            + jnp.dot(lo, b_exact, preferred_element_type=F32))


def _split_dot_rhs(a_exact, b):
    hi = b.astype(BF16)
    lo = (b - hi.astype(F32)).astype(BF16)
    return (jnp.dot(a_exact, hi, preferred_element_type=F32)
            + jnp.dot(a_exact, lo, preferred_element_type=F32))


def _sigmoid(x):
    return 1.0 / (1.0 + jnp.exp(-x))


def _gelu_exact(x):
    return 0.5 * x * (1.0 + lax.erf(x * (1.0 / math.sqrt(2.0))))


def _rms(x, g):
    return x * lax.rsqrt(jnp.mean(x * x, axis=-1, keepdims=True) + NORM_EPS) * g


def _rms_matmul_kernel(x_ref, g_ref, w_ref, o_ref, xn_ref):
    @pl.when(pl.program_id(1) == 0)
    def _():
        xn_ref[...] = _rms(x_ref[...], g_ref[...]).astype(BF16)

    o_ref[...] = jnp.dot(xn_ref[...], w_ref[...], preferred_element_type=F32)


def _rms_matmul(x, g, w, tm, tn):
    t, d = x.shape
    n = w.shape[1]
    return pl.pallas_call(
        _rms_matmul_kernel,
        grid=(t // tm, n // tn),
        in_specs=[pl.BlockSpec((tm, d), lambda i, j: (i, 0)),
                  pl.BlockSpec((1, d), lambda i, j: (0, 0)),
                  pl.BlockSpec((d, tn), lambda i, j: (0, j))],
        out_specs=pl.BlockSpec((tm, tn), lambda i, j: (i, j)),
        out_shape=jax.ShapeDtypeStruct((t, n), F32),
        scratch_shapes=[pltpu.VMEM((tm, d), BF16)],
        compiler_params=_params("parallel", "arbitrary"),
        name="rms_proj",
    )(x, g.reshape(1, d), w)


def _rwkv_kernel(z_ref, w0_ref, a0_ref, kk_ref, ka_ref, rk_ref, lnw_ref, lnb_ref,
                 w2_ref, a2_ref, g2_ref, tri_ref, bd_ref, o_ref, st_ref):
    c = RWKV_CHUNK
    n = HEAD_DIM
    w_ = RWKV_WIDTH

    @pl.when(pl.program_id(1) == 0)
    def _():
        st_ref[...] = jnp.zeros_like(st_ref)

    z = z_ref[0]
    zr = z[:, 0:w_]
    zk = z[:, w_:2 * w_]
    zv = z[:, 2 * w_:3 * w_]
    o1 = 3 * w_
    zw = z[:, o1:o1 + RWKV_DECAY_RANK]
    za = z[:, o1 + RWKV_DECAY_RANK:o1 + RWKV_DECAY_RANK + RWKV_ICLR_RANK]
    zg = z[:, o1 + RWKV_DECAY_RANK + RWKV_ICLR_RANK:]
    bd = bd_ref[...]
    tri = tri_ref[...]

    y = -(w0_ref[...] + _dot(jnp.tanh(zw), w2_ref[...]))
    softplus = jnp.maximum(y, 0.0) + jnp.log(1.0 + jnp.exp(-jnp.abs(y)))
    lw = -jnp.exp(-softplus - 0.5)
    a = _sigmoid(a0_ref[...] + _dot(za, a2_ref[...]))
    g = _dot(_sigmoid(zg), g2_ref[...])
    kk = zk * kk_ref[...]
    kk = kk * lax.rsqrt(jnp.maximum(_split_dot(kk * kk, bd), 1e-12))
    k = zk * (1.0 + (a - 1.0) * ka_ref[...])

    cum = _split_dot_rhs(tri, lw)
    e_pos = jnp.exp(cum)
    e_neg = jnp.exp(-cum)
    a_t = -kk * jnp.exp(cum - lw)
    b_t = kk * a * e_neg
    k_t = k * e_neg
    r_t = zr * e_pos

    ri = lax.broadcasted_iota(jnp.int32, (c, c), 0)
    ci = lax.broadcasted_iota(jnp.int32, (c, c), 1)
    strict = ri > ci
    incl = ri >= ci
    eye = (ri == ci).astype(F32)

    heads = range(RWKV_HEADS)
    sl = [slice(h * n, (h + 1) * n) for h in heads]
    ah = [a_t[:, s] for s in sl]
    bh = [b_t[:, s] for s in sl]
    kh = [k_t[:, s] for s in sl]
    rh = [r_t[:, s] for s in sl]
    vh = [zv[:, s] for s in sl]
    s0 = [st_ref[h] for h in heads]
    p = [jnp.where(strict, _dot_nt(ah[h], bh[h]), 0.0) for h in heads]
    l_ak = [jnp.where(strict, _dot_nt(ah[h], kh[h]), 0.0) for h in heads]
    m_rb = [jnp.where(incl, _dot_nt(rh[h], bh[h]), 0.0) for h in heads]
    m_rk = [jnp.where(incl, _dot_nt(rh[h], kh[h]), 0.0) for h in heads]
    rhs = [_dot_nt(ah[h], s0[h]) + _dot(l_ak[h], vh[h]) for h in heads]
    y0 = [_dot_nt(rh[h], s0[h]) + _dot(m_rk[h], vh[h]) for h in heads]
    s1 = [s0[h] + _dot_tn(vh[h], kh[h]) for h in heads]
    tinv = [eye + p[h] for h in heads]
    for _ in range(int(math.log2(c)) - 1):
        p = [_dot(p[h], p[h]) for h in heads]
        tinv = [tinv[h] + _dot(p[h], tinv[h]) for h in heads]
    u = [_dot(tinv[h], rhs[h]) for h in heads]
    ys = [y0[h] + _dot(m_rb[h], u[h]) for h in heads]
    for h in heads:
        st_ref[h] = (s1[h] + _dot_tn(u[h], bh[h])) * e_pos[c - 1:c, sl[h]]
    yv = jnp.concatenate(ys, axis=-1)

    mu = _split_dot(yv, bd) * (1.0 / n)
    dlt = yv - mu
    var = _split_dot(dlt * dlt, bd) * (1.0 / n)
    yn = dlt * lax.rsqrt(var + RWKV_GN_EPS) * lnw_ref[...] + lnb_ref[...]
    bonus = _split_dot(zr * k * rk_ref[...], bd) * zv
    o_ref[0] = (yn + bonus) * g


def _rwkv(rw, w0, w2, a0, a2, g2, k_k, k_a, r_k, ln_w, ln_b):
    b, s, cols = rw.shape
    c = RWKV_CHUNK
    w_ = RWKV_WIDTH
    row = lambda v: v.reshape(1, w_).astype(F32)
    idx = jnp.arange(w_) // HEAD_DIM
    bd = (idx[:, None] == idx[None, :]).astype(BF16)
    tri = (jnp.arange(c)[:, None] >= jnp.arange(c)[None, :]).astype(BF16)
    full = lambda shp: pl.BlockSpec(shp, lambda i, j: (0,) * len(shp))
    return pl.pallas_call(
        _rwkv_kernel,
        grid=(b, s // c),
        in_specs=[pl.BlockSpec((1, c, cols), lambda i, j: (i, j, 0))]
                 + [full((1, w_))] * 7
                 + [full((RWKV_DECAY_RANK, w_)), full((RWKV_ICLR_RANK, w_)), full((RWKV_GATE_RANK, w_)),
                    full((c, c)), full((w_, w_))],
        out_specs=pl.BlockSpec((1, c, w_), lambda i, j: (i, j, 0)),
        out_shape=jax.ShapeDtypeStruct((b, s, w_), F32),
        scratch_shapes=[pltpu.VMEM((RWKV_HEADS, HEAD_DIM, HEAD_DIM), F32)],
        compiler_params=_params("parallel", "arbitrary"),
        name="rwkv7_chunked",
    )(rw, row(w0), row(a0), row(k_k), row(k_a), row(r_k), row(ln_w), row(ln_b),
      w2.astype(BF16), a2.astype(BF16), g2.astype(BF16), tri, bd)


def _cmp_kernel(f_ref, pe_ref, w1_ref, w2_ref, o_ref):
    f = f_ref[0, 0] + pe_ref[0]
    h = _gelu_exact(_dot(f, w1_ref[0]))
    o_ref[0, 0] = _dot(h, w2_ref[0])


def _nsa_compress(flat, pe, w1, w2):
    two, bg, ncp, lin = flat.shape
    hid = w1.shape[-1]
    return pl.pallas_call(
        _cmp_kernel,
        grid=(two, bg),
        in_specs=[pl.BlockSpec((1, 1, ncp, lin), lambda i, j: (i, j, 0, 0)),
                  pl.BlockSpec((1, 1, lin), lambda i, j: (i, 0, 0)),
                  pl.BlockSpec((1, lin, hid), lambda i, j: (i, 0, 0)),
                  pl.BlockSpec((1, hid, HEAD_DIM), lambda i, j: (i, 0, 0))],
        out_specs=pl.BlockSpec((1, 1, ncp, HEAD_DIM), lambda i, j: (i, j, 0, 0)),
        out_shape=jax.ShapeDtypeStruct((two, bg, ncp, HEAD_DIM), F32),
        compiler_params=_params("parallel", "parallel"),
        name="nsa_compress",
    )(flat, pe, w1.astype(BF16), w2.astype(BF16))


def _attn_step(k_tile, vt_tile, qt, bias, m_ref, l_ref, acc_ref):
    s = jnp.dot(k_tile, qt, preferred_element_type=F32) + bias
    m_old = m_ref[0:1, :]
    m_new = jnp.maximum(m_old, jnp.max(s, axis=0, keepdims=True))
    p = jnp.exp(s - m_new)
    alpha = jnp.exp(m_old - m_new)
    l_ref[0:1, :] = alpha * l_ref[0:1, :] + jnp.sum(p, axis=0, keepdims=True)
    acc_ref[...] = alpha * acc_ref[...] + jnp.dot(vt_tile, p.astype(BF16), preferred_element_type=F32)
    m_ref[0:1, :] = m_new


def _key_rows(ref, j):
    return ref[0, 0, pl.ds(pl.multiple_of(j * Q_BLOCK, Q_BLOCK), Q_BLOCK), :]


def _nsa_kernel(qt_ref, gate_ref, kc_ref, vct_ref, ks_ref, vst_ref, kw_ref, vwt_ref, ovlt_ref,
                d0_ref, d1_ref, edge_ref, far_ref, o_ref, sb_ref, m_ref, l_ref, acc_ref,
                *, n_sel, n_cmp_pad):
    qb = Q_BLOCK
    hg = NSA_HG
    i = pl.program_id(2)
    s0 = i * qb
    qt = qt_ref[0, 0, 0]

    crow = lax.broadcasted_iota(jnp.int32, (n_cmp_pad, qb), 0)
    tcol = s0 + lax.broadcasted_iota(jnp.int32, (n_cmp_pad, qb), 1)
    cmask = (NSA_CMP_STRIDE * crow + NSA_CMP_LEN - 1) <= tcol
    st = jnp.dot(kc_ref[0, 0], qt, preferred_element_type=F32)
    vct = vct_ref[0, 0]
    psum = jnp.zeros((n_cmp_pad, qb), F32)
    o_cmp = []
    for h in range(hg):
        s = jnp.where(cmask, st[:, h * qb:(h + 1) * qb], NEG_BIG)
        mx = jnp.max(s, axis=0, keepdims=True)
        e = jnp.where(cmask, jnp.exp(s - mx), 0.0)
        p = e / jnp.maximum(jnp.sum(e, axis=0, keepdims=True), 1e-30)
        o_cmp.append(jnp.dot(vct, p.astype(BF16), preferred_element_type=F32))
        psum = psum + p

    imp = _split_dot_rhs(ovlt_ref[...], psum)
    nrow = lax.broadcasted_iota(jnp.int32, (n_sel, qb), 0)
    cur = lax.shift_right_logical(s0 + lax.broadcasted_iota(jnp.int32, (n_sel, qb), 1),
                                  int(math.log2(NSA_SEL_BLOCK)))
    forced = (nrow == 0) | (nrow == cur) | (nrow == cur - 1)
    score = jnp.where(forced, NSA_FORCE_SCORE, jnp.where(nrow <= cur, imp, -1.0))
    sb = jnp.full((n_sel, qb), NEG_BIG, F32)
    for _ in range(min(NSA_SEL_TOPN, n_sel)):
        mx = jnp.max(score, axis=0, keepdims=True)
        first = jnp.min(jnp.where(score == mx, nrow, n_sel), axis=0, keepdims=True)
        hit = nrow == first
        sb = jnp.where(hit, 0.0, sb)
        score = jnp.where(hit, -jnp.inf, score)
    sb_ref[...] = sb

    blocks_per_tile = qb // NSA_SEL_BLOCK

    def sel_bias(j0, ntiles):
        rows = [jnp.broadcast_to(sb_ref[pl.ds(blocks_per_tile * j0 + r, 1), :], (NSA_SEL_BLOCK, qb))
                for r in range(blocks_per_tile * ntiles)]
        return jnp.concatenate([jnp.concatenate(rows, axis=0)] * hg, axis=1)

    def reset():
        m_ref[...] = jnp.full_like(m_ref, NEG_BIG)
        l_ref[...] = jnp.zeros_like(l_ref)
        acc_ref[...] = jnp.zeros_like(acc_ref)

    def finish():
        return acc_ref[...] / jnp.maximum(l_ref[0:1, :], 1e-30)

    def step(k_ref, vt_ref, j0, ntiles, bias):
        k_rows = k_ref[0, 0, pl.ds(pl.multiple_of(j0 * qb, qb), ntiles * qb), :]
        vt = jnp.concatenate([vt_ref[0, 0, j0 + t] for t in range(ntiles)], axis=1)
        _attn_step(k_rows, vt, qt, bias, m_ref, l_ref, acc_ref)

    def near_step(k_ref, vt_ref, selected):
        @pl.when(i >= 1)
        def _():
            bias = jnp.concatenate([d1_ref[0], d0_ref[0]], axis=0)
            step(k_ref, vt_ref, i - 1, 2, bias + sel_bias(i - 1, 2) if selected else bias)

        @pl.when(i == 0)
        def _():
            bias = d0_ref[0]
            step(k_ref, vt_ref, 0, 1, bias + sel_bias(0, 1) if selected else bias)

    far = far_ref[0, 0:1, :]
    far_group = 4

    reset()
    near_step(ks_ref, vst_ref, True)
    n_far = jnp.maximum(i - 1, 0)
    n_grp = lax.shift_right_logical(n_far, int(math.log2(far_group)))

    def far_group_body(gi, carry):
        step(ks_ref, vst_ref, gi * far_group, far_group, sel_bias(gi * far_group, far_group) + far)
        return carry

    def far_tile_body(j, carry):
        step(ks_ref, vst_ref, j, 1, sel_bias(j, 1) + far)
        return carry

    lax.fori_loop(0, n_grp, far_group_body, 0)
    lax.fori_loop(n_grp * far_group, n_far, far_tile_body, 0)
    o_sel = finish()

    reset()
    near_step(kw_ref, vwt_ref, False)
    n_back = NSA_WINDOW // qb
    n_mid = n_back - 2

    @pl.when(i >= n_back - 1)
    def _():
        step(kw_ref, vwt_ref, i - (n_back - 1), n_mid, far)

    for d in range(2, n_back):
        @pl.when((i >= d) & (i < n_back - 1))
        def _(d=d):
            step(kw_ref, vwt_ref, i - d, 1, far)

    @pl.when(i >= n_back)
    def _():
        step(kw_ref, vwt_ref, i - n_back, 1, edge_ref[...] + far)
    o_win = finish()

    gs = _sigmoid(gate_ref[0, 0, 0])
    o_ref[0, 0, 0] = (gs[0:1, :] * jnp.concatenate(o_cmp, axis=1) + gs[1:2, :] * o_sel + gs[2:3, :] * o_win)


def _t5_bucket(dist):
    n = jnp.maximum(dist, 0)
    max_exact = REL_BUCKETS // 2
    nf = jnp.maximum(n, max_exact).astype(F32)
    large = max_exact + (jnp.log(nf / max_exact) / math.log(REL_MAX_DIST / max_exact)
                         * (REL_BUCKETS - max_exact)).astype(jnp.int32)
    large = jnp.minimum(large, REL_BUCKETS - 1)
    return jnp.where(n < max_exact, n, large)


def _bias_tiles(tbl, groups, hg):
    qb = Q_BLOCK
    kj = jnp.arange(qb)[:, None]
    qi = jnp.arange(qb)[None, :]
    t = tbl.T.reshape(groups, hg, REL_BUCKETS).astype(F32)
    lay = lambda x: jnp.transpose(x, (0, 2, 1, 3)).reshape(groups, qb, hg * qb)
    buckets = jnp.arange(REL_BUCKETS)

    def lookup(bucket):
        hit = bucket[None, None, :, :, None] == buckets
        return jnp.sum(jnp.where(hit, t[:, :, None, None, :], 0.0), axis=-1)

    d0 = lay(lookup(_t5_bucket(qi - kj)) + jnp.where(kj <= qi, 0.0, NEG_BIG))
    d1 = lay(lookup(_t5_bucket(qi - kj + qb)))
    edge = jnp.tile(jnp.where(qi < kj, 0.0, NEG_BIG).astype(F32), (1, hg))
    far = jnp.broadcast_to(t[:, :, REL_BUCKETS - 1][:, None, :, None], (groups, 8, hg, qb))
    return d0, d1, edge, far.reshape(groups, 8, hg * qb)


def _tiles_t(z, b, s, g):
    nt = s // Q_BLOCK
    z = z.reshape(b, nt, Q_BLOCK, g, HEAD_DIM)
    return jnp.transpose(z, (0, 3, 1, 4, 2)).astype(BF16)


def _rows(z, b, s, g):
    return jnp.transpose(z.reshape(b, s, g, HEAD_DIM), (0, 2, 1, 3)).astype(BF16)


def _q_t(q, b, s, g, hg):
    nq = s // Q_BLOCK
    q = q.reshape(b, nq, Q_BLOCK, g, hg, HEAD_DIM) * (HEAD_DIM ** -0.5)
    return jnp.transpose(q, (0, 3, 1, 5, 4, 2)).reshape(b, g, nq, HEAD_DIM, hg * Q_BLOCK).astype(BF16)


def _o_from_t(o_t, b, s, g, hg):
    nq = s // Q_BLOCK
    o = o_t.reshape(b, g, nq, HEAD_DIM, hg, Q_BLOCK)
    return jnp.transpose(o, (0, 2, 5, 1, 4, 3)).reshape(b, s, g * hg * HEAD_DIM)


def _nsa(q, kc, vc, ks, vs, kw, vw, gates, pe_k, pe_v, ck_w1, ck_w2, cv_w1, cv_w2, tbl):
    b, s, _ = q.shape
    g, hg, dh = NSA_KV_GROUPS, NSA_HG, HEAD_DIM
    qb = Q_BLOCK
    nchunk = s // NSA_CMP_STRIDE
    n_sub = NSA_CMP_LEN // NSA_CMP_STRIDE
    n_cmp = nchunk - n_sub + 1
    n_sel = s // NSA_SEL_BLOCK
    nt = s // qb

    def flat_blocks(z):
        ch = z.reshape(b, nchunk, NSA_CMP_STRIDE, g, dh)
        ch = jnp.pad(ch, ((0, 0), (0, n_sub - 1), (0, 0), (0, 0), (0, 0)))
        blk = jnp.concatenate([ch[:, j:j + nchunk] for j in range(n_sub)], axis=2)
        return jnp.transpose(blk, (0, 3, 1, 2, 4)).reshape(b * g, nchunk, NSA_CMP_LEN * dh)

    flat = jnp.stack([flat_blocks(kc), flat_blocks(vc)])
    pe = jnp.stack([pe_k.reshape(1, -1), pe_v.reshape(1, -1)])
    cmp_kv = _nsa_compress(flat, pe, jnp.stack([ck_w1, cv_w1]), jnp.stack([ck_w2, cv_w2]))
    cmp_kv = cmp_kv.reshape(2, b, g, nchunk, dh)
    kcm = cmp_kv[0].astype(BF16)
    vct = jnp.transpose(cmp_kv[1], (0, 1, 3, 2)).astype(BF16)

    cmp_start = jnp.arange(nchunk) * NSA_CMP_STRIDE
    cmp_end = cmp_start + NSA_CMP_LEN - 1
    sel_start = jnp.arange(n_sel) * NSA_SEL_BLOCK
    ovlt = ((cmp_end[None, :] >= sel_start[:, None])
            & (cmp_start[None, :] <= sel_start[:, None] + NSA_SEL_BLOCK - 1)
            & (jnp.arange(nchunk)[None, :] < n_cmp)).astype(BF16)

    d0, d1, edge, far = tbl
    gt = gates.reshape(b, nt, qb, g, hg, 3)
    gt = jnp.transpose(gt, (0, 3, 1, 5, 4, 2)).reshape(b, g, nt, 3, hg * qb)
    gt = jnp.pad(gt, ((0, 0), (0, 0), (0, 0), (0, 5), (0, 0)))

    wide = hg * qb
    per_q = lambda rows: pl.BlockSpec((1, 1, 1, rows, wide), lambda bi, gi, i: (bi, gi, i, 0, 0))
    rows_spec = pl.BlockSpec((1, 1, s, dh), lambda bi, gi, i: (bi, gi, 0, 0))
    tiles_spec = pl.BlockSpec((1, 1, nt, dh, qb), lambda bi, gi, i: (bi, gi, 0, 0, 0))
    bias_spec = pl.BlockSpec((1, qb, wide), lambda bi, gi, i: (gi, 0, 0))
    o_t = pl.pallas_call(
        functools.partial(_nsa_kernel, n_sel=n_sel, n_cmp_pad=nchunk),
        grid=(b, g, nt),
        in_specs=[per_q(dh), per_q(8),
                  pl.BlockSpec((1, 1, nchunk, dh), lambda bi, gi, i: (bi, gi, 0, 0)),
                  pl.BlockSpec((1, 1, dh, nchunk), lambda bi, gi, i: (bi, gi, 0, 0)),
                  rows_spec, tiles_spec, rows_spec, tiles_spec,
                  pl.BlockSpec((n_sel, nchunk), lambda bi, gi, i: (0, 0)),
                  bias_spec, bias_spec,
                  pl.BlockSpec((qb, wide), lambda bi, gi, i: (0, 0)),
                  pl.BlockSpec((1, 8, wide), lambda bi, gi, i: (gi, 0, 0))],
        out_specs=per_q(dh),
        out_shape=jax.ShapeDtypeStruct((b, g, nt, dh, wide), F32),
        scratch_shapes=[pltpu.VMEM((n_sel, qb), F32), pltpu.VMEM((8, wide), F32),
                        pltpu.VMEM((8, wide), F32), pltpu.VMEM((dh, wide), F32)],
        compiler_params=_params("parallel", "parallel", "arbitrary"),
        name="nsa_attention",
    )(_q_t(q, b, s, g, hg), gt, kcm, vct,
      _rows(ks, b, s, g), _tiles_t(vs, b, s, g), _rows(kw, b, s, g), _tiles_t(vw, b, s, g),
      ovlt, d0, d1, edge, far)
    return _o_from_t(o_t, b, s, g, hg)


def _swa_kernel(qt_ref, k_ref, vt_ref, d0_ref, d1_ref, sink_ref, o_ref, m_ref, l_ref, acc_ref):
    i = pl.program_id(2)
    qt = qt_ref[0, 0, 0]
    m_ref[...] = sink_ref[0]
    l_ref[...] = jnp.ones_like(l_ref)
    acc_ref[...] = jnp.zeros_like(acc_ref)
    _attn_step(_key_rows(k_ref, i), vt_ref[0, 0, i], qt, d0_ref[0], m_ref, l_ref, acc_ref)

    @pl.when(i >= 1)
    def _():
        _attn_step(_key_rows(k_ref, i - 1), vt_ref[0, 0, i - 1], qt, d1_ref[0], m_ref, l_ref, acc_ref)

    o_ref[0, 0, 0] = acc_ref[...] / l_ref[0:1, :]


def _swa(q, k, v, sinks, tbl):
    b, s, _ = q.shape
    g, hg, dh = SWA_KV_HEADS, SWA_HG, HEAD_DIM
    qb = Q_BLOCK
    nt = s // qb
    assert SWA_WINDOW == qb
    d0, d1, edge, _ = tbl
    d1 = d1 + edge[None]
    wide = hg * qb
    sink = jnp.broadcast_to(sinks.reshape(g, 1, hg, 1).astype(F32), (g, 8, hg, qb)).reshape(g, 8, wide)
    per_q = pl.BlockSpec((1, 1, 1, dh, wide), lambda bi, gi, i: (bi, gi, i, 0, 0))
    bias_spec = pl.BlockSpec((1, qb, wide), lambda bi, gi, i: (gi, 0, 0))
    o_t = pl.pallas_call(
        _swa_kernel,
        grid=(b, g, nt),
        in_specs=[per_q,
                  pl.BlockSpec((1, 1, s, dh), lambda bi, gi, i: (bi, gi, 0, 0)),
                  pl.BlockSpec((1, 1, nt, dh, qb), lambda bi, gi, i: (bi, gi, 0, 0, 0)),
                  bias_spec, bias_spec,
                  pl.BlockSpec((1, 8, wide), lambda bi, gi, i: (gi, 0, 0))],
        out_specs=per_q,
        out_shape=jax.ShapeDtypeStruct((b, g, nt, dh, wide), F32),
        scratch_shapes=[pltpu.VMEM((8, wide), F32), pltpu.VMEM((8, wide), F32), pltpu.VMEM((dh, wide), F32)],
        compiler_params=_params("parallel", "parallel", "arbitrary"),
        name="swa_sink_attention",
    )(_q_t(q, b, s, g, hg), _rows(k, b, s, g), _tiles_t(v, b, s, g), d0, d1, sink)
    return _o_from_t(o_t, b, s, g, hg)


def _merge_kernel(x_ref, oa_ref, ob_ref, oc_ref, ga_ref, gb_ref, gc_ref,
                  wa_ref, wb_ref, wc_ref, wo_ref, o_ref):
    merged = (_sigmoid(ga_ref[...]) * _dot(oa_ref[...], wa_ref[...])
              + _sigmoid(gb_ref[...]) * _dot(ob_ref[...], wb_ref[...])
              + _sigmoid(gc_ref[...]) * _dot(oc_ref[...], wc_ref[...]))
    o_ref[...] = x_ref[...] + _dot(merged, wo_ref[...])


def _merge(x, o_a, o_b, o_c, g_a, g_b, g_c, w_a, w_b, w_c, w_o, tm):
    t, d = x.shape
    tok = lambda w: pl.BlockSpec((tm, w), lambda i: (i, 0))
    full = lambda a: pl.BlockSpec(a.shape, lambda i: (0, 0))
    ws = [w.astype(BF16) for w in (w_a, w_b, w_c, w_o)]
    return pl.pallas_call(
        _merge_kernel,
        grid=(t // tm,),
        in_specs=[tok(d), tok(o_a.shape[1]), tok(o_b.shape[1]), tok(o_c.shape[1]), tok(d), tok(d), tok(d)]
                 + [full(w) for w in ws],
        out_specs=tok(d),
        out_shape=jax.ShapeDtypeStruct((t, d), F32),
        compiler_params=_params("parallel"),
        name="branch_merge",
    )(x, o_a, o_b, o_c, g_a, g_b, g_c, *ws)


def _top_rows(x, k):
    r = x.shape[0]
    ridx = lax.broadcasted_iota(jnp.int32, x.shape, 0)
    vals = []
    for _ in range(k):
        mx = jnp.max(x, axis=0, keepdims=True)
        first = jnp.min(jnp.where(x == mx, ridx, r), axis=0, keepdims=True)
        x = jnp.where(ridx == first, -jnp.inf, x)
        vals.append(mx)
    return jnp.concatenate(vals, axis=0)


def _peer_score_kernel(x_ref, g_ref, wqt_ref, sk_ref, z_ref, s1_ref, s2_ref, e1_ref, e2_ref, tau_ref):
    kk = PEER_TOPK
    half = PEER_QDIM // 2
    tt = x_ref.shape[0]
    lane_chunk = PEER_LANE_CHUNK
    zb = _rms(x_ref[...], g_ref[...]).astype(BF16)
    z_ref[...] = zb
    q_t = lax.dot_general(wqt_ref[...], zb, (((1,), (1,)), ((), ())), preferred_element_type=F32)
    q_t = q_t.astype(BF16)
    for h in range(PEER_HEADS):
        s = [jnp.dot(sk_ref[h, p], q_t[(2 * h + p) * half:(2 * h + p + 1) * half, :],
                     preferred_element_type=F32) for p in range(2)]
        top = [_top_rows(s[p], kk) for p in range(2)]
        pieces = [top[0][i:i + 1] + top[1][0:kk // (i + 1)] for i in range(kk)]
        n_cand = sum(pc.shape[0] for pc in pieces)
        pieces.append(jnp.full(((-n_cand) % 8, tt), -jnp.inf, F32))
        best = _top_rows(jnp.concatenate(pieces, axis=0), kk)
        zsum = jnp.sum(jnp.exp(best - best[0:1]), axis=0, keepdims=True)
        e1 = jnp.exp(s[0] - top[0][0:1]) / zsum
        e2 = jnp.exp(s[1] - top[1][0:1])
        tau = jnp.broadcast_to(best[kk - 1:kk], (8, tt))
        for c in range(tt // lane_chunk):
            lanes = slice(c * lane_chunk, (c + 1) * lane_chunk)
            s1_ref[h, c] = s[0][:, lanes]
            s2_ref[h, c] = s[1][:, lanes]
            e1_ref[h, c] = e1[:, lanes]
            e2_ref[h, c] = e2[:, lanes]
            tau_ref[h, c] = tau[:, lanes]


def _peer_scores(x, g, wq, subkeys, tt):
    t, d = x.shape
    hp, nk = PEER_HEADS, PEER_NKEYS
    wqt = jnp.transpose(wq).astype(BF16)
    sk = subkeys.astype(BF16)
    lc = PEER_LANE_CHUNK
    stat = lambda rows: pl.BlockSpec((hp, tt // lc, rows, lc), lambda i: (0, i, 0, 0))
    shp = lambda rows: jax.ShapeDtypeStruct((hp, t // lc, rows, lc), F32)
    return pl.pallas_call(
        _peer_score_kernel,
        grid=(t // tt,),
        in_specs=[pl.BlockSpec((tt, d), lambda i: (i, 0)),
                  pl.BlockSpec((1, d), lambda i: (0, 0)),
                  pl.BlockSpec(wqt.shape, lambda i: (0, 0)),
                  pl.BlockSpec(sk.shape, lambda i: (0, 0, 0, 0))],
        out_specs=[pl.BlockSpec((tt, d), lambda i: (i, 0)), stat(nk), stat(nk), stat(nk), stat(nk), stat(8)],
        out_shape=[jax.ShapeDtypeStruct((t, d), BF16), shp(nk), shp(nk), shp(nk), shp(nk), shp(8)],
        compiler_params=_params("parallel"),
        name="peer_scores",
    )(x, g.reshape(1, d), wqt, sk)


def _peer_main_kernel(x_ref, z_ref, s1_ref, s2_ref, e1_ref, e2_ref, tau_ref, u_ref, v_ref, lnf_ref,
                      o_ref, acc_ref, w_ref, *, rows_per_tile, final_norm):
    e = pl.program_id(1)

    @pl.when(e == 0)
    def _():
        acc_ref[...] = jnp.zeros_like(acc_ref)

    nk = PEER_NKEYS
    n_chunks = s2_ref.shape[1]
    row_shift = int(math.log2(rows_per_tile))
    h_t = _dot_nt(u_ref[...], z_ref[...])

    def weight_block(it, carry):
        c = lax.shift_right_logical(it, row_shift)
        r = jnp.bitwise_and(it, rows_per_tile - 1)
        w = jnp.zeros((nk, PEER_LANE_CHUNK), F32)
        for h in range(PEER_HEADS):
            pair = s2_ref[h, c] + s1_ref[h, c, pl.ds(r, 1), :]
            val = e2_ref[h, c] * e1_ref[h, c, pl.ds(r, 1), :]
            w = w + jnp.where(pair >= tau_ref[h, c, 0:1, :], val, 0.0)
        w_ref[c, pl.ds(pl.multiple_of(r * nk, nk), nk), :] = w
        return carry

    lax.fori_loop(0, n_chunks * rows_per_tile, weight_block, 0)
    act = jnp.concatenate([w_ref[c] for c in range(n_chunks)], axis=1) * _gelu_exact(h_t)
    acc_ref[...] += _dot_tn(act, v_ref[...])

    @pl.when(e == pl.num_programs(1) - 1)
    def _():
        y = x_ref[...] + acc_ref[...]
        if final_norm:
            y = _rms(y, lnf_ref[...])
        o_ref[...] = y


def _peer_main(x, z, s1, s2, e1, e2, tau, u_tab, v_tab, lnf_w, final_norm, tt, te):
    t, d = x.shape
    hp, nk = PEER_HEADS, PEER_NKEYS
    n_exp = u_tab.shape[0]
    lc = PEER_LANE_CHUNK
    stat = lambda rows: pl.BlockSpec((hp, tt // lc, rows, lc), lambda i, e: (0, i, 0, 0))
    rows_per_tile = te // nk
    assert rows_per_tile == 8
    tile_rows = pl.BlockSpec((hp, tt // lc, rows_per_tile, lc), lambda i, e: (0, i, e, 0))
    return pl.pallas_call(
        functools.partial(_peer_main_kernel, rows_per_tile=rows_per_tile, final_norm=final_norm),
        grid=(t // tt, n_exp // te),
        in_specs=[pl.BlockSpec((tt, d), lambda i, e: (i, 0)),
                  pl.BlockSpec((tt, d), lambda i, e: (i, 0)),
                  tile_rows, stat(nk), tile_rows, stat(nk), stat(8),
                  pl.BlockSpec((te, d), lambda i, e: (e, 0)),
                  pl.BlockSpec((te, d), lambda i, e: (e, 0)),
                  pl.BlockSpec((1, d), lambda i, e: (0, 0))],
        out_specs=pl.BlockSpec((tt, d), lambda i, e: (i, 0)),
        out_shape=jax.ShapeDtypeStruct((t, d), F32),
        scratch_shapes=[pltpu.VMEM((tt, d), F32), pltpu.VMEM((tt // lc, te, lc), F32)],
        compiler_params=_params("parallel", "arbitrary"),
        name="peer_dense",
    )(x, z, s1, s2, e1, e2, tau, u_tab.astype(BF16), v_tab.astype(BF16), lnf_w.reshape(1, d))


def _pad_cols(w, mult):
    pad = (-w.shape[1]) % mult
    return jnp.pad(w, ((0, 0), (0, pad))) if pad else w


def _split(z, sizes):
    out, o = [], 0
    for sz in sizes:
        out.append(z[..., o:o + sz])
        o += sz
    return out


def _layer(x, b, s, p, l, nsa_tbl, swa_tbl, lnf_w, final_norm):
    t, d = x.shape
    hd = HEAD_DIM
    proj = _rms_matmul(x, p["ln1_w"][l], _pad_cols(p["w_in"][l], 512).astype(BF16), 512, 512)
    nsa_sizes = (NSA_Q_HEADS * hd,) + (NSA_KV_GROUPS * hd,) * 6 + (NSA_Q_HEADS * 3,)
    swa_sizes = (SWA_Q_HEADS * hd, SWA_KV_HEADS * hd, SWA_KV_HEADS * hd)
    cols = _split(proj, (RWKV_COLS,) + nsa_sizes + swa_sizes + (d, d, d))
    rw = cols[0].reshape(b, s, RWKV_COLS)
    rw_prev = jnp.pad(rw, ((0, 0), (1, 0), (0, 0)))[:, :-1]
    rw = rw + p["rwkv_mu"][l] * (rw_prev - rw)
    o_a = _rwkv(rw, p["rwkv_w0"][l], p["rwkv_w2"][l], p["rwkv_a0"][l], p["rwkv_a2"][l], p["rwkv_g2"][l],
                p["rwkv_k_k"][l], p["rwkv_k_a"][l], p["rwkv_r_k"][l], p["rwkv_ln_w"][l], p["rwkv_ln_b"][l])
    seq = lambda z: z.reshape(b, s, z.shape[-1])
    nq, nkc, nvc, nks, nvs, nkw, nvw, ngate = [seq(c) for c in cols[1:9]]
    o_b = _nsa(nq, nkc, nvc, nks, nvs, nkw, nvw, ngate, p["nsa_pe_k"][l], p["nsa_pe_v"][l],
               p["nsa_ck_w1"][l], p["nsa_ck_w2"][l], p["nsa_cv_w1"][l], p["nsa_cv_w2"][l], nsa_tbl)
    sq, sk, sv = [seq(c) for c in cols[9:12]]
    o_c = _swa(sq, sk, sv, p["swa_sinks"][l], swa_tbl)
    g_a, g_b, g_c = cols[12:15]
    x = _merge(x, o_a.reshape(t, -1), o_b.reshape(t, -1), o_c.reshape(t, -1), g_a, g_b, g_c,
               p["w_br_a"][l], p["w_br_b"][l], p["w_br_c"][l], p["w_out"][l], 256)
    z, s1, s2, e1, e2, tau = _peer_scores(x, p["ln2_w"][l], p["peer_wq"][l], p["peer_subkeys"][l], 256)
    return _peer_main(x, z, s1, s2, e1, e2, tau, p["peer_u"][l], p["peer_v"][l], lnf_w, final_norm, 512, 1024)


def kernel(x, ln1_w, ln2_w, lnf_w, rel_bias, w_in, rwkv_mu, rwkv_w0, rwkv_w2, rwkv_a0, rwkv_a2, rwkv_g2, rwkv_k_k, rwkv_k_a, rwkv_r_k, rwkv_ln_w, rwkv_ln_b, nsa_pe_k, nsa_pe_v, nsa_ck_w1, nsa_ck_w2, nsa_cv_w1, nsa_cv_w2, swa_sinks, w_br_a, w_br_b, w_br_c, w_out, peer_wq, peer_subkeys, peer_u, peer_v):
    p = dict(ln1_w=ln1_w, ln2_w=ln2_w, w_in=w_in, rwkv_mu=rwkv_mu, rwkv_w0=rwkv_w0, rwkv_w2=rwkv_w2,
             rwkv_a0=rwkv_a0, rwkv_a2=rwkv_a2, rwkv_g2=rwkv_g2, rwkv_k_k=rwkv_k_k, rwkv_k_a=rwkv_k_a,
             rwkv_r_k=rwkv_r_k.reshape(rwkv_r_k.shape[0], -1), rwkv_ln_w=rwkv_ln_w, rwkv_ln_b=rwkv_ln_b,
             nsa_pe_k=nsa_pe_k, nsa_pe_v=nsa_pe_v, nsa_ck_w1=nsa_ck_w1, nsa_ck_w2=nsa_ck_w2,
             nsa_cv_w1=nsa_cv_w1, nsa_cv_w2=nsa_cv_w2, swa_sinks=swa_sinks, w_br_a=w_br_a, w_br_b=w_br_b,
             w_br_c=w_br_c, w_out=w_out, peer_wq=peer_wq, peer_subkeys=peer_subkeys, peer_u=peer_u,
             peer_v=peer_v)
    b, s, d = x.shape
    depth = w_in.shape[0]
    nsa_tbl = _bias_tiles(rel_bias[:, :NSA_Q_HEADS], NSA_KV_GROUPS, NSA_HG)
    swa_tbl = _bias_tiles(rel_bias[:, NSA_Q_HEADS:], SWA_KV_HEADS, SWA_HG)
    y = x.reshape(b * s, d)
    for l in range(depth):
        y = _layer(y, b, s, p, l, nsa_tbl, swa_tbl, lnf_w, l == depth - 1)
    return y.reshape(b, s, d)
```

```python
import functools
import math

import jax
import jax.numpy as jnp
from jax import lax
from jax.experimental import pallas as pl
from jax.experimental.pallas import tpu as pltpu

F32 = jnp.float32
BF16 = jnp.bfloat16

D_MODEL = 1024
HEAD_DIM = 64
Q_BLOCK = 128
NORM_EPS = 1e-6

RWKV_HEADS = 8
RWKV_WIDTH = RWKV_HEADS * HEAD_DIM
RWKV_DECAY_RANK = 64
RWKV_ICLR_RANK = 64
RWKV_GATE_RANK = 128
RWKV_GN_EPS = 64e-5
RWKV_COLS = 3 * RWKV_WIDTH + RWKV_DECAY_RANK + RWKV_ICLR_RANK + RWKV_GATE_RANK
RWKV_CHUNK = 64

NSA_Q_HEADS = 8
NSA_KV_GROUPS = 2
NSA_HG = NSA_Q_HEADS // NSA_KV_GROUPS
NSA_CMP_LEN = 32
NSA_CMP_STRIDE = 16
NSA_CMP_HIDDEN = 256
NSA_SEL_BLOCK = 64
NSA_SEL_TOPN = 16
NSA_WINDOW = 512
NSA_FORCE_SCORE = 1e9

SWA_Q_HEADS = 8
SWA_KV_HEADS = 2
SWA_HG = SWA_Q_HEADS // SWA_KV_HEADS
SWA_WINDOW = 128

REL_BUCKETS = 32
REL_MAX_DIST = 128

PEER_HEADS = 8
PEER_NKEYS = 128
PEER_QDIM = 256
PEER_TOPK = 16
PEER_LANE_CHUNK = 128

NEG_BIG = -1e30
VMEM_LIMIT_BYTES = 56 * 1024 * 1024


def _params(*sem):
    return pltpu.CompilerParams(dimension_semantics=sem, vmem_limit_bytes=VMEM_LIMIT_BYTES)


def _dot(a, b):
    return jnp.dot(a.astype(BF16), b.astype(BF16), preferred_element_type=F32)


def _dot_nt(a, b):
    return lax.dot_general(a.astype(BF16), b.astype(BF16), (((1,), (1,)), ((), ())),
                           preferred_element_type=F32)


def _dot_tn(a, b):
    return lax.dot_general(a.astype(BF16), b.astype(BF16), (((0,), (0,)), ((), ())),
                           preferred_element_type=F32)


def _split_dot(a, b_exact):
    hi = a.astype(BF16)
    lo = (a - hi.astype(F32)).astype(BF16)
    return (jnp.dot(hi, b_exact, preferred_element_type=F32)
            + jnp.dot(lo, b_exact, preferred_element_type=F32))


def _split_dot_rhs(a_exact, b):
    hi = b.astype(BF16)
    lo = (b - hi.astype(F32)).astype(BF16)
    return (jnp.dot(a_exact, hi, preferred_element_type=F32)
            + jnp.dot(a_exact, lo, preferred_element_type=F32))


def _sigmoid(x):
    return 1.0 / (1.0 + jnp.exp(-x))


def _gelu_exact(x):
    return 0.5 * x * (1.0 + lax.erf(x * (1.0 / math.sqrt(2.0))))


def _rms(x, g):
    return x * lax.rsqrt(jnp.mean(x * x, axis=-1, keepdims=True) + NORM_EPS) * g


def _rms_matmul_kernel(x_ref, g_ref, w_ref, o_ref, xn_ref):
    @pl.when(pl.program_id(1) == 0)
    def _():
        xn_ref[...] = _rms(x_ref[...], g_ref[...]).astype(BF16)

    o_ref[...] = jnp.dot(xn_ref[...], w_ref[...], preferred_element_type=F32)


def _rms_matmul(x, g, w, tm, tn):
    t, d = x.shape
    n = w.shape[1]
    return pl.pallas_call(
        _rms_matmul_kernel,
        grid=(t // tm, n // tn),
        in_specs=[pl.BlockSpec((tm, d), lambda i, j: (i, 0)),
                  pl.BlockSpec((1, d), lambda i, j: (0, 0)),
                  pl.BlockSpec((d, tn), lambda i, j: (0, j))],
        out_specs=pl.BlockSpec((tm, tn), lambda i, j: (i, j)),
        out_shape=jax.ShapeDtypeStruct((t, n), F32),
        scratch_shapes=[pltpu.VMEM((tm, d), BF16)],
        compiler_params=_params("parallel", "arbitrary"),
        name="rms_proj",
    )(x, g.reshape(1, d), w)


def _rwkv_kernel(z_ref, w0_ref, a0_ref, kk_ref, ka_ref, rk_ref, lnw_ref, lnb_ref,
                 w2_ref, a2_ref, g2_ref, tri_ref, bd_ref, o_ref, st_ref):
    c = RWKV_CHUNK
    n = HEAD_DIM
    w_ = RWKV_WIDTH

    @pl.when(pl.program_id(1) == 0)
    def _():
        st_ref[...] = jnp.zeros_like(st_ref)

    z = z_ref[0]
    zr = z[:, 0:w_]
    zk = z[:, w_:2 * w_]
    zv = z[:, 2 * w_:3 * w_]
    o1 = 3 * w_
    zw = z[:, o1:o1 + RWKV_DECAY_RANK]
    za = z[:, o1 + RWKV_DECAY_RANK:o1 + RWKV_DECAY_RANK + RWKV_ICLR_RANK]
    zg = z[:, o1 + RWKV_DECAY_RANK + RWKV_ICLR_RANK:]
    bd = bd_ref[...]
    tri = tri_ref[...]

    y = -(w0_ref[...] + _dot(jnp.tanh(zw), w2_ref[...]))
    softplus = jnp.maximum(y, 0.0) + jnp.log(1.0 + jnp.exp(-jnp.abs(y)))
    lw = -jnp.exp(-softplus - 0.5)
    a = _sigmoid(a0_ref[...] + _dot(za, a2_ref[...]))
    g = _dot(_sigmoid(zg), g2_ref[...])
    kk = zk * kk_ref[...]
    kk = kk * lax.rsqrt(jnp.maximum(_split_dot(kk * kk, bd), 1e-12))
    k = zk * (1.0 + (a - 1.0) * ka_ref[...])

    cum = _split_dot_rhs(tri, lw)
    e_pos = jnp.exp(cum)
    e_neg = jnp.exp(-cum)
    a_t = -kk * jnp.exp(cum - lw)
    b_t = kk * a * e_neg
    k_t = k * e_neg
    r_t = zr * e_pos

    ri = lax.broadcasted_iota(jnp.int32, (c, c), 0)
    ci = lax.broadcasted_iota(jnp.int32, (c, c), 1)
    strict = ri > ci
    incl = ri >= ci
    eye = (ri == ci).astype(F32)

    heads = range(RWKV_HEADS)
    sl = [slice(h * n, (h + 1) * n) for h in heads]
    ah = [a_t[:, s] for s in sl]
    bh = [b_t[:, s] for s in sl]
    kh = [k_t[:, s] for s in sl]
    rh = [r_t[:, s] for s in sl]
    vh = [zv[:, s] for s in sl]
    s0 = [st_ref[h] for h in heads]
    p = [jnp.where(strict, _dot_nt(ah[h], bh[h]), 0.0) for h in heads]
    l_ak = [jnp.where(strict, _dot_nt(ah[h], kh[h]), 0.0) for h in heads]
    m_rb = [jnp.where(incl, _dot_nt(rh[h], bh[h]), 0.0) for h in heads]
    m_rk = [jnp.where(incl, _dot_nt(rh[h], kh[h]), 0.0) for h in heads]
    rhs = [_dot_nt(ah[h], s0[h]) + _dot(l_ak[h], vh[h]) for h in heads]
    y0 = [_dot_nt(rh[h], s0[h]) + _dot(m_rk[h], vh[h]) for h in heads]
    s1 = [s0[h] + _dot_tn(vh[h], kh[h]) for h in heads]
    tinv = [eye + p[h] for h in heads]
    for _ in range(int(math.log2(c)) - 1):
        p = [_dot(p[h], p[h]) for h in heads]
        tinv = [tinv[h] + _dot(p[h], tinv[h]) for h in heads]
    u = [_dot(tinv[h], rhs[h]) for h in heads]
    ys = [y0[h] + _dot(m_rb[h], u[h]) for h in heads]
    for h in heads:
        st_ref[h] = (s1[h] + _dot_tn(u[h], bh[h])) * e_pos[c - 1:c, sl[h]]
    yv = jnp.concatenate(ys, axis=-1)

    mu = _split_dot(yv, bd) * (1.0 / n)
    dlt = yv - mu
    var = _split_dot(dlt * dlt, bd) * (1.0 / n)
    yn = dlt * lax.rsqrt(var + RWKV_GN_EPS) * lnw_ref[...] + lnb_ref[...]
    bonus = _split_dot(zr * k * rk_ref[...], bd) * zv
    o_ref[0] = (yn + bonus) * g


def _rwkv(rw, w0, w2, a0, a2, g2, k_k, k_a, r_k, ln_w, ln_b):
    b, s, cols = rw.shape
    c = RWKV_CHUNK
    w_ = RWKV_WIDTH
    row = lambda v: v.reshape(1, w_).astype(F32)
    idx = jnp.arange(w_) // HEAD_DIM
    bd = (idx[:, None] == idx[None, :]).astype(BF16)
    tri = (jnp.arange(c)[:, None] >= jnp.arange(c)[None, :]).astype(BF16)
    full = lambda shp: pl.BlockSpec(shp, lambda i, j: (0,) * len(shp))
    return pl.pallas_call(
        _rwkv_kernel,
        grid=(b, s // c),
        in_specs=[pl.BlockSpec((1, c, cols), lambda i, j: (i, j, 0))]
                 + [full((1, w_))] * 7
                 + [full((RWKV_DECAY_RANK, w_)), full((RWKV_ICLR_RANK, w_)), full((RWKV_GATE_RANK, w_)),
                    full((c, c)), full((w_, w_))],
        out_specs=pl.BlockSpec((1, c, w_), lambda i, j: (i, j, 0)),
        out_shape=jax.ShapeDtypeStruct((b, s, w_), F32),
        scratch_shapes=[pltpu.VMEM((RWKV_HEADS, HEAD_DIM, HEAD_DIM), F32)],
        compiler_params=_params("parallel", "arbitrary"),
        name="rwkv7_chunked",
    )(rw, row(w0), row(a0), row(k_k), row(k_a), row(r_k), row(ln_w), row(ln_b),
      w2.astype(BF16), a2.astype(BF16), g2.astype(BF16), tri, bd)


def _cmp_kernel(f_ref, pe_ref, w1_ref, w2_ref, o_ref):
    f = f_ref[0, 0] + pe_ref[0]
    h = _gelu_exact(_dot(f, w1_ref[0]))
    o_ref[0, 0] = _dot(h, w2_ref[0])


def _nsa_compress(flat, pe, w1, w2):
    two, bg, ncp, lin = flat.shape
    hid = w1.shape[-1]
    return pl.pallas_call(
        _cmp_kernel,
        grid=(two, bg),
        in_specs=[pl.BlockSpec((1, 1, ncp, lin), lambda i, j: (i, j, 0, 0)),
                  pl.BlockSpec((1, 1, lin), lambda i, j: (i, 0, 0)),
                  pl.BlockSpec((1, lin, hid), lambda i, j: (i, 0, 0)),
                  pl.BlockSpec((1, hid, HEAD_DIM), lambda i, j: (i, 0, 0))],
        out_specs=pl.BlockSpec((1, 1, ncp, HEAD_DIM), lambda i, j: (i, j, 0, 0)),
        out_shape=jax.ShapeDtypeStruct((two, bg, ncp, HEAD_DIM), F32),
        compiler_params=_params("parallel", "parallel"),
        name="nsa_compress",
    )(flat, pe, w1.astype(BF16), w2.astype(BF16))


def _attn_step(k_tile, vt_tile, qt, bias, m_ref, l_ref, acc_ref):
    s = jnp.dot(k_tile, qt, preferred_element_type=F32) + bias
    m_old = m_ref[0:1, :]
    m_new = jnp.maximum(m_old, jnp.max(s, axis=0, keepdims=True))
    p = jnp.exp(s - m_new)
    alpha = jnp.exp(m_old - m_new)
    l_ref[0:1, :] = alpha * l_ref[0:1, :] + jnp.sum(p, axis=0, keepdims=True)
    acc_ref[...] = alpha * acc_ref[...] + jnp.dot(vt_tile, p.astype(BF16), preferred_element_type=F32)
    m_ref[0:1, :] = m_new


def _key_rows(ref, j):
    return ref[0, 0, pl.ds(pl.multiple_of(j * Q_BLOCK, Q_BLOCK), Q_BLOCK), :]


def _nsa_kernel(qt_ref, gate_ref, kc_ref, vct_ref, ks_ref, vst_ref, kw_ref, vwt_ref, ovlt_ref,
                d0_ref, d1_ref, edge_ref, far_ref, o_ref, sb_ref, m_ref, l_ref, acc_ref,
                *, n_sel, n_cmp_pad):
    qb = Q_BLOCK
    hg = NSA_HG
    i = pl.program_id(2)
    s0 = i * qb
    qt = qt_ref[0, 0, 0]

    crow = lax.broadcasted_iota(jnp.int32, (n_cmp_pad, qb), 0)
    tcol = s0 + lax.broadcasted_iota(jnp.int32, (n_cmp_pad, qb), 1)
    cmask = (NSA_CMP_STRIDE * crow + NSA_CMP_LEN - 1) <= tcol
    st = jnp.dot(kc_ref[0, 0], qt, preferred_element_type=F32)
    vct = vct_ref[0, 0]
    psum = jnp.zeros((n_cmp_pad, qb), F32)
    o_cmp = []
    for h in range(hg):
        s = jnp.where(cmask, st[:, h * qb:(h + 1) * qb], NEG_BIG)
        mx = jnp.max(s, axis=0, keepdims=True)
        e = jnp.where(cmask, jnp.exp(s - mx), 0.0)
        p = e / jnp.maximum(jnp.sum(e, axis=0, keepdims=True), 1e-30)
        o_cmp.append(jnp.dot(vct, p.astype(BF16), preferred_element_type=F32))
        psum = psum + p

    imp = _split_dot_rhs(ovlt_ref[...], psum)
    nrow = lax.broadcasted_iota(jnp.int32, (n_sel, qb), 0)
    cur = lax.shift_right_logical(s0 + lax.broadcasted_iota(jnp.int32, (n_sel, qb), 1),
                                  int(math.log2(NSA_SEL_BLOCK)))
    forced = (nrow == 0) | (nrow == cur) | (nrow == cur - 1)
    score = jnp.where(forced, NSA_FORCE_SCORE, jnp.where(nrow <= cur, imp, -1.0))
    sb = jnp.full((n_sel, qb), NEG_BIG, F32)
    for _ in range(min(NSA_SEL_TOPN, n_sel)):
        mx = jnp.max(score, axis=0, keepdims=True)
        first = jnp.min(jnp.where(score == mx, nrow, n_sel), axis=0, keepdims=True)
        hit = nrow == first
        sb = jnp.where(hit, 0.0, sb)
        score = jnp.where(hit, -jnp.inf, score)
    sb_ref[...] = sb

    blocks_per_tile = qb // NSA_SEL_BLOCK

    def sel_bias(j0, ntiles):
        rows = [jnp.broadcast_to(sb_ref[pl.ds(blocks_per_tile * j0 + r, 1), :], (NSA_SEL_BLOCK, qb))
                for r in range(blocks_per_tile * ntiles)]
        return jnp.concatenate([jnp.concatenate(rows, axis=0)] * hg, axis=1)

    def reset():
        m_ref[...] = jnp.full_like(m_ref, NEG_BIG)
        l_ref[...] = jnp.zeros_like(l_ref)
        acc_ref[...] = jnp.zeros_like(acc_ref)

    def finish():
        return acc_ref[...] / jnp.maximum(l_ref[0:1, :], 1e-30)

    def step(k_ref, vt_ref, j0, ntiles, bias):
        k_rows = k_ref[0, 0, pl.ds(pl.multiple_of(j0 * qb, qb), ntiles * qb), :]
        vt = jnp.concatenate([vt_ref[0, 0, j0 + t] for t in range(ntiles)], axis=1)
        _attn_step(k_rows, vt, qt, bias, m_ref, l_ref, acc_ref)

    def near_step(k_ref, vt_ref, selected):
        @pl.when(i >= 1)
        def _():
            bias = jnp.concatenate([d1_ref[0], d0_ref[0]], axis=0)
            step(k_ref, vt_ref, i - 1, 2, bias + sel_bias(i - 1, 2) if selected else bias)

        @pl.when(i == 0)
        def _():
            bias = d0_ref[0]
            step(k_ref, vt_ref, 0, 1, bias + sel_bias(0, 1) if selected else bias)

    far = far_ref[0, 0:1, :]
    far_group = 8
    half_group = far_group // 2

    reset()
    near_step(ks_ref, vst_ref, True)
    n_far = jnp.maximum(i - 1, 0)
    n_grp = lax.shift_right_logical(n_far, int(math.log2(far_group)))
    done = n_grp * far_group
    take_half = (n_far - done) >= half_group

    def far_group_body(gi, carry):
        step(ks_ref, vst_ref, gi * far_group, far_group, sel_bias(gi * far_group, far_group) + far)
        return carry

    def far_tile_body(j, carry):
        step(ks_ref, vst_ref, j, 1, sel_bias(j, 1) + far)
        return carry

    lax.fori_loop(0, n_grp, far_group_body, 0)

    @pl.when(take_half)
    def _():
        step(ks_ref, vst_ref, done, half_group, sel_bias(done, half_group) + far)

    lax.fori_loop(done + jnp.where(take_half, half_group, 0), n_far, far_tile_body, 0)
    o_sel = finish()

    reset()
    near_step(kw_ref, vwt_ref, False)
    n_back = NSA_WINDOW // qb
    n_mid = n_back - 2

    @pl.when(i >= n_back - 1)
    def _():
        step(kw_ref, vwt_ref, i - (n_back - 1), n_mid, far)

    for d in range(2, n_back):
        @pl.when((i >= d) & (i < n_back - 1))
        def _(d=d):
            step(kw_ref, vwt_ref, i - d, 1, far)

    @pl.when(i >= n_back)
    def _():
        step(kw_ref, vwt_ref, i - n_back, 1, edge_ref[...] + far)
    o_win = finish()

    gs = _sigmoid(gate_ref[0, 0, 0])
    o_ref[0, 0, 0] = (gs[0:1, :] * jnp.concatenate(o_cmp, axis=1) + gs[1:2, :] * o_sel + gs[2:3, :] * o_win)


def _t5_bucket(dist):
    n = jnp.maximum(dist, 0)
    max_exact = REL_BUCKETS // 2
    nf = jnp.maximum(n, max_exact).astype(F32)
    large = max_exact + (jnp.log(nf / max_exact) / math.log(REL_MAX_DIST / max_exact)
                         * (REL_BUCKETS - max_exact)).astype(jnp.int32)
    large = jnp.minimum(large, REL_BUCKETS - 1)
    return jnp.where(n < max_exact, n, large)


def _bias_tiles(tbl, groups, hg):
    qb = Q_BLOCK
    kj = jnp.arange(qb)[:, None]
    qi = jnp.arange(qb)[None, :]
    t = tbl.T.reshape(groups, hg, REL_BUCKETS).astype(F32)
    lay = lambda x: jnp.transpose(x, (0, 2, 1, 3)).reshape(groups, qb, hg * qb)
    buckets = jnp.arange(REL_BUCKETS)

    def lookup(bucket):
        hit = bucket[None, None, :, :, None] == buckets
        return jnp.sum(jnp.where(hit, t[:, :, None, None, :], 0.0), axis=-1)

    d0 = lay(lookup(_t5_bucket(qi - kj)) + jnp.where(kj <= qi, 0.0, NEG_BIG))
    d1 = lay(lookup(_t5_bucket(qi - kj + qb)))
    edge = jnp.tile(jnp.where(qi < kj, 0.0, NEG_BIG).astype(F32), (1, hg))
    far = jnp.broadcast_to(t[:, :, REL_BUCKETS - 1][:, None, :, None], (groups, 8, hg, qb))
    return d0, d1, edge, far.reshape(groups, 8, hg * qb)


def _tiles_t(z, b, s, g):
    nt = s // Q_BLOCK
    z = z.reshape(b, nt, Q_BLOCK, g, HEAD_DIM)
    return jnp.transpose(z, (0, 3, 1, 4, 2)).astype(BF16)


def _rows(z, b, s, g):
    return jnp.transpose(z.reshape(b, s, g, HEAD_DIM), (0, 2, 1, 3)).astype(BF16)


def _q_t(q, b, s, g, hg):
    nq = s // Q_BLOCK
    q = q.reshape(b, nq, Q_BLOCK, g, hg, HEAD_DIM) * (HEAD_DIM ** -0.5)
    return jnp.transpose(q, (0, 3, 1, 5, 4, 2)).reshape(b, g, nq, HEAD_DIM, hg * Q_BLOCK).astype(BF16)


def _o_from_t(o_t, b, s, g, hg):
    nq = s // Q_BLOCK
    o = o_t.reshape(b, g, nq, HEAD_DIM, hg, Q_BLOCK)
    return jnp.transpose(o, (0, 2, 5, 1, 4, 3)).reshape(b, s, g * hg * HEAD_DIM)


def _nsa(q, kc, vc, ks, vs, kw, vw, gates, pe_k, pe_v, ck_w1, ck_w2, cv_w1, cv_w2, tbl):
    b, s, _ = q.shape
    g, hg, dh = NSA_KV_GROUPS, NSA_HG, HEAD_DIM
    qb = Q_BLOCK
    nchunk = s // NSA_CMP_STRIDE
    n_sub = NSA_CMP_LEN // NSA_CMP_STRIDE
    n_cmp = nchunk - n_sub + 1
    n_sel = s // NSA_SEL_BLOCK
    nt = s // qb

    def flat_blocks(z):
        ch = z.reshape(b, nchunk, NSA_CMP_STRIDE, g, dh)
        ch = jnp.pad(ch, ((0, 0), (0, n_sub - 1), (0, 0), (0, 0), (0, 0)))
        blk = jnp.concatenate([ch[:, j:j + nchunk] for j in range(n_sub)], axis=2)
        return jnp.transpose(blk, (0, 3, 1, 2, 4)).reshape(b * g, nchunk, NSA_CMP_LEN * dh)

    flat = jnp.stack([flat_blocks(kc), flat_blocks(vc)])
    pe = jnp.stack([pe_k.reshape(1, -1), pe_v.reshape(1, -1)])
    cmp_kv = _nsa_compress(flat, pe, jnp.stack([ck_w1, cv_w1]), jnp.stack([ck_w2, cv_w2]))
    cmp_kv = cmp_kv.reshape(2, b, g, nchunk, dh)
    kcm = cmp_kv[0].astype(BF16)
    vct = jnp.transpose(cmp_kv[1], (0, 1, 3, 2)).astype(BF16)

    cmp_start = jnp.arange(nchunk) * NSA_CMP_STRIDE
    cmp_end = cmp_start + NSA_CMP_LEN - 1
    sel_start = jnp.arange(n_sel) * NSA_SEL_BLOCK
    ovlt = ((cmp_end[None, :] >= sel_start[:, None])
            & (cmp_start[None, :] <= sel_start[:, None] + NSA_SEL_BLOCK - 1)
            & (jnp.arange(nchunk)[None, :] < n_cmp)).astype(BF16)

    d0, d1, edge, far = tbl
    gt = gates.reshape(b, nt, qb, g, hg, 3)
    gt = jnp.transpose(gt, (0, 3, 1, 5, 4, 2)).reshape(b, g, nt, 3, hg * qb)
    gt = jnp.pad(gt, ((0, 0), (0, 0), (0, 0), (0, 5), (0, 0)))

    wide = hg * qb
    per_q = lambda rows: pl.BlockSpec((1, 1, 1, rows, wide), lambda bi, gi, i: (bi, gi, i, 0, 0))
    rows_spec = pl.BlockSpec((1, 1, s, dh), lambda bi, gi, i: (bi, gi, 0, 0))
    tiles_spec = pl.BlockSpec((1, 1, nt, dh, qb), lambda bi, gi, i: (bi, gi, 0, 0, 0))
    bias_spec = pl.BlockSpec((1, qb, wide), lambda bi, gi, i: (gi, 0, 0))
    o_t = pl.pallas_call(
        functools.partial(_nsa_kernel, n_sel=n_sel, n_cmp_pad=nchunk),
        grid=(b, g, nt),
        in_specs=[per_q(dh), per_q(8),
                  pl.BlockSpec((1, 1, nchunk, dh), lambda bi, gi, i: (bi, gi, 0, 0)),
                  pl.BlockSpec((1, 1, dh, nchunk), lambda bi, gi, i: (bi, gi, 0, 0)),
                  rows_spec, tiles_spec, rows_spec, tiles_spec,
                  pl.BlockSpec((n_sel, nchunk), lambda bi, gi, i: (0, 0)),
                  bias_spec, bias_spec,
                  pl.BlockSpec((qb, wide), lambda bi, gi, i: (0, 0)),
                  pl.BlockSpec((1, 8, wide), lambda bi, gi, i: (gi, 0, 0))],
        out_specs=per_q(dh),
        out_shape=jax.ShapeDtypeStruct((b, g, nt, dh, wide), F32),
        scratch_shapes=[pltpu.VMEM((n_sel, qb), F32), pltpu.VMEM((8, wide), F32),
                        pltpu.VMEM((8, wide), F32), pltpu.VMEM((dh, wide), F32)],
        compiler_params=_params("parallel", "parallel", "arbitrary"),
        name="nsa_attention",
    )(_q_t(q, b, s, g, hg), gt, kcm, vct,
      _rows(ks, b, s, g), _tiles_t(vs, b, s, g), _rows(kw, b, s, g), _tiles_t(vw, b, s, g),
      ovlt, d0, d1, edge, far)
    return _o_from_t(o_t, b, s, g, hg)


def _swa_kernel(qt_ref, k_ref, vt_ref, d0_ref, d1_ref, sink_ref, o_ref, m_ref, l_ref, acc_ref):
    i = pl.program_id(2)
    qt = qt_ref[0, 0, 0]
    m_ref[...] = sink_ref[0]
    l_ref[...] = jnp.ones_like(l_ref)
    acc_ref[...] = jnp.zeros_like(acc_ref)
    _attn_step(_key_rows(k_ref, i), vt_ref[0, 0, i], qt, d0_ref[0], m_ref, l_ref, acc_ref)

    @pl.when(i >= 1)
    def _():
        _attn_step(_key_rows(k_ref, i - 1), vt_ref[0, 0, i - 1], qt, d1_ref[0], m_ref, l_ref, acc_ref)

    o_ref[0, 0, 0] = acc_ref[...] / l_ref[0:1, :]


def _swa(q, k, v, sinks, tbl):
    b, s, _ = q.shape
    g, hg, dh = SWA_KV_HEADS, SWA_HG, HEAD_DIM
    qb = Q_BLOCK
    nt = s // qb
    assert SWA_WINDOW == qb
    d0, d1, edge, _ = tbl
    d1 = d1 + edge[None]
    wide = hg * qb
    sink = jnp.broadcast_to(sinks.reshape(g, 1, hg, 1).astype(F32), (g, 8, hg, qb)).reshape(g, 8, wide)
    per_q = pl.BlockSpec((1, 1, 1, dh, wide), lambda bi, gi, i: (bi, gi, i, 0, 0))
    bias_spec = pl.BlockSpec((1, qb, wide), lambda bi, gi, i: (gi, 0, 0))
    o_t = pl.pallas_call(
        _swa_kernel,
        grid=(b, g, nt),
        in_specs=[per_q,
                  pl.BlockSpec((1, 1, s, dh), lambda bi, gi, i: (bi, gi, 0, 0)),
                  pl.BlockSpec((1, 1, nt, dh, qb), lambda bi, gi, i: (bi, gi, 0, 0, 0)),
                  bias_spec, bias_spec,
                  pl.BlockSpec((1, 8, wide), lambda bi, gi, i: (gi, 0, 0))],
        out_specs=per_q,
        out_shape=jax.ShapeDtypeStruct((b, g, nt, dh, wide), F32),
        scratch_shapes=[pltpu.VMEM((8, wide), F32), pltpu.VMEM((8, wide), F32), pltpu.VMEM((dh, wide), F32)],
        compiler_params=_params("parallel", "parallel", "arbitrary"),
        name="swa_sink_attention",
    )(_q_t(q, b, s, g, hg), _rows(k, b, s, g), _tiles_t(v, b, s, g), d0, d1, sink)
    return _o_from_t(o_t, b, s, g, hg)


def _merge_kernel(x_ref, oa_ref, ob_ref, oc_ref, ga_ref, gb_ref, gc_ref,
                  wa_ref, wb_ref, wc_ref, wo_ref, o_ref):
    merged = (_sigmoid(ga_ref[...]) * _dot(oa_ref[...], wa_ref[...])
              + _sigmoid(gb_ref[...]) * _dot(ob_ref[...], wb_ref[...])
              + _sigmoid(gc_ref[...]) * _dot(oc_ref[...], wc_ref[...]))
    o_ref[...] = x_ref[...] + _dot(merged, wo_ref[...])


def _merge(x, o_a, o_b, o_c, g_a, g_b, g_c, w_a, w_b, w_c, w_o, tm):
    t, d = x.shape
    tok = lambda w: pl.BlockSpec((tm, w), lambda i: (i, 0))
    full = lambda a: pl.BlockSpec(a.shape, lambda i: (0, 0))
    ws = [w.astype(BF16) for w in (w_a, w_b, w_c, w_o)]
    return pl.pallas_call(
        _merge_kernel,
        grid=(t // tm,),
        in_specs=[tok(d), tok(o_a.shape[1]), tok(o_b.shape[1]), tok(o_c.shape[1]), tok(d), tok(d), tok(d)]
                 + [full(w) for w in ws],
        out_specs=tok(d),
        out_shape=jax.ShapeDtypeStruct((t, d), F32),
        compiler_params=_params("parallel"),
        name="branch_merge",
    )(x, o_a, o_b, o_c, g_a, g_b, g_c, *ws)


def _top_rows(x, k):
    r = x.shape[0]
    ridx = lax.broadcasted_iota(jnp.int32, x.shape, 0)
    vals = []
    for _ in range(k):
        mx = jnp.max(x, axis=0, keepdims=True)
        first = jnp.min(jnp.where(x == mx, ridx, r), axis=0, keepdims=True)
        x = jnp.where(ridx == first, -jnp.inf, x)
        vals.append(mx)
    return jnp.concatenate(vals, axis=0)


def _peer_score_kernel(x_ref, g_ref, wqt_ref, sk_ref, z_ref, s1_ref, s2_ref, e1_ref, e2_ref, tau_ref):
    kk = PEER_TOPK
    half = PEER_QDIM // 2
    tt = x_ref.shape[0]
    lane_chunk = PEER_LANE_CHUNK
    zb = _rms(x_ref[...], g_ref[...]).astype(BF16)
    z_ref[...] = zb
    q_t = lax.dot_general(wqt_ref[...], zb, (((1,), (1,)), ((), ())), preferred_element_type=F32)
    q_t = q_t.astype(BF16)
    for h in range(PEER_HEADS):
        s = [jnp.dot(sk_ref[h, p], q_t[(2 * h + p) * half:(2 * h + p + 1) * half, :],
                     preferred_element_type=F32) for p in range(2)]
        top = [_top_rows(s[p], kk) for p in range(2)]
        pieces = [top[0][i:i + 1] + top[1][0:kk // (i + 1)] for i in range(kk)]
        n_cand = sum(pc.shape[0] for pc in pieces)
        pieces.append(jnp.full(((-n_cand) % 8, tt), -jnp.inf, F32))
        best = _top_rows(jnp.concatenate(pieces, axis=0), kk)
        zsum = jnp.sum(jnp.exp(best - best[0:1]), axis=0, keepdims=True)
        e1 = jnp.exp(s[0] - top[0][0:1]) / zsum
        e2 = jnp.exp(s[1] - top[1][0:1])
        tau = jnp.broadcast_to(best[kk - 1:kk], (8, tt))
        for c in range(tt // lane_chunk):
            lanes = slice(c * lane_chunk, (c + 1) * lane_chunk)
            s1_ref[h, c] = s[0][:, lanes]
            s2_ref[h, c] = s[1][:, lanes]
            e1_ref[h, c] = e1[:, lanes]
            e2_ref[h, c] = e2[:, lanes]
            tau_ref[h, c] = tau[:, lanes]


def _peer_scores(x, g, wq, subkeys, tt):
    t, d = x.shape
    hp, nk = PEER_HEADS, PEER_NKEYS
    wqt = jnp.transpose(wq).astype(BF16)
    sk = subkeys.astype(BF16)
    lc = PEER_LANE_CHUNK
    stat = lambda rows: pl.BlockSpec((hp, tt // lc, rows, lc), lambda i: (0, i, 0, 0))
    shp = lambda rows: jax.ShapeDtypeStruct((hp, t // lc, rows, lc), F32)
    return pl.pallas_call(
        _peer_score_kernel,
        grid=(t // tt,),
        in_specs=[pl.BlockSpec((tt, d), lambda i: (i, 0)),
                  pl.BlockSpec((1, d), lambda i: (0, 0)),
                  pl.BlockSpec(wqt.shape, lambda i: (0, 0)),
                  pl.BlockSpec(sk.shape, lambda i: (0, 0, 0, 0))],
        out_specs=[pl.BlockSpec((tt, d), lambda i: (i, 0)), stat(nk), stat(nk), stat(nk), stat(nk), stat(8)],
        out_shape=[jax.ShapeDtypeStruct((t, d), BF16), shp(nk), shp(nk), shp(nk), shp(nk), shp(8)],
        compiler_params=_params("parallel"),
        name="peer_scores",
    )(x, g.reshape(1, d), wqt, sk)


def _peer_main_kernel(x_ref, z_ref, s1_ref, s2_ref, e1_ref, e2_ref, tau_ref, u_ref, v_ref, lnf_ref,
                      o_ref, acc_ref, w_ref, *, rows_per_tile, final_norm):
    e = pl.program_id(1)

    @pl.when(e == 0)
    def _():
        acc_ref[...] = jnp.zeros_like(acc_ref)

    nk = PEER_NKEYS
    n_chunks = s2_ref.shape[1]
    row_shift = int(math.log2(rows_per_tile))
    h_t = _dot_nt(u_ref[...], z_ref[...])

    def weight_block(it, carry):
        c = lax.shift_right_logical(it, row_shift)
        r = jnp.bitwise_and(it, rows_per_tile - 1)
        w = jnp.zeros((nk, PEER_LANE_CHUNK), F32)
        for h in range(PEER_HEADS):
            pair = s2_ref[h, c] + s1_ref[h, c, pl.ds(r, 1), :]
            val = e2_ref[h, c] * e1_ref[h, c, pl.ds(r, 1), :]
            w = w + jnp.where(pair >= tau_ref[h, c, 0:1, :], val, 0.0)
        w_ref[c, pl.ds(pl.multiple_of(r * nk, nk), nk), :] = w
        return carry

    lax.fori_loop(0, n_chunks * rows_per_tile, weight_block, 0, unroll=2)
    act = jnp.concatenate([w_ref[c] for c in range(n_chunks)], axis=1) * _gelu_exact(h_t)
    acc_ref[...] += _dot_tn(act, v_ref[...])

    @pl.when(e == pl.num_programs(1) - 1)
    def _():
        y = x_ref[...] + acc_ref[...]
        if final_norm:
            y = _rms(y, lnf_ref[...])
        o_ref[...] = y


def _peer_main(x, z, s1, s2, e1, e2, tau, u_tab, v_tab, lnf_w, final_norm, tt, te):
    t, d = x.shape
    hp, nk = PEER_HEADS, PEER_NKEYS
    n_exp = u_tab.shape[0]
    lc = PEER_LANE_CHUNK
    stat = lambda rows: pl.BlockSpec((hp, tt // lc, rows, lc), lambda i, e: (0, i, 0, 0))
    rows_per_tile = te // nk
    assert rows_per_tile == 8
    tile_rows = pl.BlockSpec((hp, tt // lc, rows_per_tile, lc), lambda i, e: (0, i, e, 0))
    return pl.pallas_call(
        functools.partial(_peer_main_kernel, rows_per_tile=rows_per_tile, final_norm=final_norm),
        grid=(t // tt, n_exp // te),
        in_specs=[pl.BlockSpec((tt, d), lambda i, e: (i, 0)),
                  pl.BlockSpec((tt, d), lambda i, e: (i, 0)),
                  tile_rows, stat(nk), tile_rows, stat(nk), stat(8),
                  pl.BlockSpec((te, d), lambda i, e: (e, 0)),
                  pl.BlockSpec((te, d), lambda i, e: (e, 0)),
                  pl.BlockSpec((1, d), lambda i, e: (0, 0))],
        out_specs=pl.BlockSpec((tt, d), lambda i, e: (i, 0)),
        out_shape=jax.ShapeDtypeStruct((t, d), F32),
        scratch_shapes=[pltpu.VMEM((tt, d), F32), pltpu.VMEM((tt // lc, te, lc), F32)],
        compiler_params=_params("parallel", "arbitrary"),
        name="peer_dense",
    )(x, z, s1, s2, e1, e2, tau, u_tab.astype(BF16), v_tab.astype(BF16), lnf_w.reshape(1, d))


def _pad_cols(w, mult):
    pad = (-w.shape[1]) % mult
    return jnp.pad(w, ((0, 0), (0, pad))) if pad else w


def _split(z, sizes):
    out, o = [], 0
    for sz in sizes:
        out.append(z[..., o:o + sz])
        o += sz
    return out


def _layer(x, b, s, p, l, nsa_tbl, swa_tbl, lnf_w, final_norm):
    t, d = x.shape
    hd = HEAD_DIM
    proj = _rms_matmul(x, p["ln1_w"][l], _pad_cols(p["w_in"][l], 512).astype(BF16), 512, 512)
    nsa_sizes = (NSA_Q_HEADS * hd,) + (NSA_KV_GROUPS * hd,) * 6 + (NSA_Q_HEADS * 3,)
    swa_sizes = (SWA_Q_HEADS * hd, SWA_KV_HEADS * hd, SWA_KV_HEADS * hd)
    cols = _split(proj, (RWKV_COLS,) + nsa_sizes + swa_sizes + (d, d, d))
    rw = cols[0].reshape(b, s, RWKV_COLS)
    rw_prev = jnp.pad(rw, ((0, 0), (1, 0), (0, 0)))[:, :-1]
    rw = rw + p["rwkv_mu"][l] * (rw_prev - rw)
    o_a = _rwkv(rw, p["rwkv_w0"][l], p["rwkv_w2"][l], p["rwkv_a0"][l], p["rwkv_a2"][l], p["rwkv_g2"][l],
                p["rwkv_k_k"][l], p["rwkv_k_a"][l], p["rwkv_r_k"][l], p["rwkv_ln_w"][l], p["rwkv_ln_b"][l])
    seq = lambda z: z.reshape(b, s, z.shape[-1])
    nq, nkc, nvc, nks, nvs, nkw, nvw, ngate = [seq(c) for c in cols[1:9]]
    o_b = _nsa(nq, nkc, nvc, nks, nvs, nkw, nvw, ngate, p["nsa_pe_k"][l], p["nsa_pe_v"][l],
               p["nsa_ck_w1"][l], p["nsa_ck_w2"][l], p["nsa_cv_w1"][l], p["nsa_cv_w2"][l], nsa_tbl)
    sq, sk, sv = [seq(c) for c in cols[9:12]]
    o_c = _swa(sq, sk, sv, p["swa_sinks"][l], swa_tbl)
    g_a, g_b, g_c = cols[12:15]
    x = _merge(x, o_a.reshape(t, -1), o_b.reshape(t, -1), o_c.reshape(t, -1), g_a, g_b, g_c,
               p["w_br_a"][l], p["w_br_b"][l], p["w_br_c"][l], p["w_out"][l], 256)
    z, s1, s2, e1, e2, tau = _peer_scores(x, p["ln2_w"][l], p["peer_wq"][l], p["peer_subkeys"][l], 256)
    return _peer_main(x, z, s1, s2, e1, e2, tau, p["peer_u"][l], p["peer_v"][l], lnf_w, final_norm, 512, 1024)


def kernel(x, ln1_w, ln2_w, lnf_w, rel_bias, w_in, rwkv_mu, rwkv_w0, rwkv_w2, rwkv_a0, rwkv_a2, rwkv_g2, rwkv_k_k, rwkv_k_a, rwkv_r_k, rwkv_ln_w, rwkv_ln_b, nsa_pe_k, nsa_pe_v, nsa_ck_w1, nsa_ck_w2, nsa_cv_w1, nsa_cv_w2, swa_sinks, w_br_a, w_br_b, w_br_c, w_out, peer_wq, peer_subkeys, peer_u, peer_v):
    p = dict(ln1_w=ln1_w, ln2_w=ln2_w, w_in=w_in, rwkv_mu=rwkv_mu, rwkv_w0=rwkv_w0, rwkv_w2=rwkv_w2,
             rwkv_a0=rwkv_a0, rwkv_a2=rwkv_a2, rwkv_g2=rwkv_g2, rwkv_k_k=rwkv_k_k, rwkv_k_a=rwkv_k_a,
             rwkv_r_k=rwkv_r_k.reshape(rwkv_r_k.shape[0], -1), rwkv_ln_w=rwkv_ln_w, rwkv_ln_b=rwkv_ln_b,
             nsa_pe_k=nsa_pe_k, nsa_pe_v=nsa_pe_v, nsa_ck_w1=nsa_ck_w1, nsa_ck_w2=nsa_ck_w2,
             nsa_cv_w1=nsa_cv_w1, nsa_cv_w2=nsa_cv_w2, swa_sinks=swa_sinks, w_br_a=w_br_a, w_br_b=w_br_b,
             w_br_c=w_br_c, w_out=w_out, peer_wq=peer_wq, peer_subkeys=peer_subkeys, peer_u=peer_u,
             peer_v=peer_v)
    b, s, d = x.shape
    depth = w_in.shape[0]
    nsa_tbl = _bias_tiles(rel_bias[:, :NSA_Q_HEADS], NSA_KV_GROUPS, NSA_HG)
    swa_tbl = _bias_tiles(rel_bias[:, NSA_Q_HEADS:], SWA_KV_HEADS, SWA_HG)
    y = x.reshape(b * s, d)
    for l in range(depth):
        y = _layer(y, b, s, p, l, nsa_tbl, swa_tbl, lnf_w, l == depth - 1)
    return y.reshape(b, s, d)
```

```python
import functools
import math

import jax
import jax.numpy as jnp
from jax import lax
from jax.experimental import pallas as pl
from jax.experimental.pallas import tpu as pltpu

F32 = jnp.float32
BF16 = jnp.bfloat16

D_MODEL = 1024
HEAD_DIM = 64
Q_BLOCK = 128
NORM_EPS = 1e-6

RWKV_HEADS = 8
RWKV_WIDTH = RWKV_HEADS * HEAD_DIM
RWKV_DECAY_RANK = 64
RWKV_ICLR_RANK = 64
RWKV_GATE_RANK = 128
RWKV_GN_EPS = 64e-5
RWKV_COLS = 3 * RWKV_WIDTH + RWKV_DECAY_RANK + RWKV_ICLR_RANK + RWKV_GATE_RANK
RWKV_CHUNK = 64

NSA_Q_HEADS = 8
NSA_KV_GROUPS = 2
NSA_HG = NSA_Q_HEADS // NSA_KV_GROUPS
NSA_CMP_LEN = 32
NSA_CMP_STRIDE = 16
NSA_CMP_HIDDEN = 256
NSA_SEL_BLOCK = 64
NSA_SEL_TOPN = 16
NSA_WINDOW = 512
NSA_FORCE_SCORE = 1e9

SWA_Q_HEADS = 8
SWA_KV_HEADS = 2
SWA_HG = SWA_Q_HEADS // SWA_KV_HEADS
SWA_WINDOW = 128

REL_BUCKETS = 32
REL_MAX_DIST = 128

PEER_HEADS = 8
PEER_NKEYS = 128
PEER_QDIM = 256
PEER_TOPK = 16
PEER_LANE_CHUNK = 128

NEG_BIG = -1e30
VMEM_LIMIT_BYTES = 56 * 1024 * 1024


def _params(*sem):
    return pltpu.CompilerParams(dimension_semantics=sem, vmem_limit_bytes=VMEM_LIMIT_BYTES)


def _dot(a, b):
    return jnp.dot(a.astype(BF16), b.astype(BF16), preferred_element_type=F32)


def _dot_nt(a, b):
    return lax.dot_general(a.astype(BF16), b.astype(BF16), (((1,), (1,)), ((), ())),
                           preferred_element_type=F32)


def _dot_tn(a, b):
    return lax.dot_general(a.astype(BF16), b.astype(BF16), (((0,), (0,)), ((), ())),
                           preferred_element_type=F32)


def _split_dot(a, b_exact):
    hi = a.astype(BF16)
    lo = (a - hi.astype(F32)).astype(BF16)
    return (jnp.dot(hi, b_exact, preferred_element_type=F32)
            + jnp.dot(lo, b_exact, preferred_element_type=F32))


def _split_dot_rhs(a_exact, b):
    hi = b.astype(BF16)
    lo = (b - hi.astype(F32)).astype(BF16)
    return (jnp.dot(a_exact, hi, preferred_element_type=F32)
            + jnp.dot(a_exact, lo, preferred_element_type=F32))


def _sigmoid(x):
    return 1.0 / (1.0 + jnp.exp(-x))


def _gelu_exact(x):
    return 0.5 * x * (1.0 + lax.erf(x * (1.0 / math.sqrt(2.0))))


def _rms(x, g):
    return x * lax.rsqrt(jnp.mean(x * x, axis=-1, keepdims=True) + NORM_EPS) * g


def _rms_matmul_kernel(x_ref, g_ref, w_ref, o_ref, xn_ref):
    @pl.when(pl.program_id(1) == 0)
    def _():
        xn_ref[...] = _rms(x_ref[...], g_ref[...]).astype(BF16)

    o_ref[...] = jnp.dot(xn_ref[...], w_ref[...], preferred_element_type=F32)


def _rms_matmul(x, g, w, tm, tn):
    t, d = x.shape
    n = w.shape[1]
    return pl.pallas_call(
        _rms_matmul_kernel,
        grid=(t // tm, n // tn),
        in_specs=[pl.BlockSpec((tm, d), lambda i, j: (i, 0)),
                  pl.BlockSpec((1, d), lambda i, j: (0, 0)),
                  pl.BlockSpec((d, tn), lambda i, j: (0, j))],
        out_specs=pl.BlockSpec((tm, tn), lambda i, j: (i, j)),
        out_shape=jax.ShapeDtypeStruct((t, n), F32),
        scratch_shapes=[pltpu.VMEM((tm, d), BF16)],
        compiler_params=_params("parallel", "arbitrary"),
        name="rms_proj",
    )(x, g.reshape(1, d), w)


def _rwkv_kernel(z_ref, w0_ref, a0_ref, kk_ref, ka_ref, rk_ref, lnw_ref, lnb_ref,
                 w2_ref, a2_ref, g2_ref, tri_ref, bd_ref, o_ref, st_ref):
    c = RWKV_CHUNK
    n = HEAD_DIM
    w_ = RWKV_WIDTH

    @pl.when(pl.program_id(1) == 0)
    def _():
        st_ref[...] = jnp.zeros_like(st_ref)

    z = z_ref[0]
    zr = z[:, 0:w_]
    zk = z[:, w_:2 * w_]
    zv = z[:, 2 * w_:3 * w_]
    o1 = 3 * w_
    zw = z[:, o1:o1 + RWKV_DECAY_RANK]
    za = z[:, o1 + RWKV_DECAY_RANK:o1 + RWKV_DECAY_RANK + RWKV_ICLR_RANK]
    zg = z[:, o1 + RWKV_DECAY_RANK + RWKV_ICLR_RANK:]
    bd = bd_ref[...]
    tri = tri_ref[...]

    y = -(w0_ref[...] + _dot(jnp.tanh(zw), w2_ref[...]))
    softplus = jnp.maximum(y, 0.0) + jnp.log(1.0 + jnp.exp(-jnp.abs(y)))
    lw = -jnp.exp(-softplus - 0.5)
    a = _sigmoid(a0_ref[...] + _dot(za, a2_ref[...]))
    g = _dot(_sigmoid(zg), g2_ref[...])
    kk = zk * kk_ref[...]
    kk = kk * lax.rsqrt(jnp.maximum(_split_dot(kk * kk, bd), 1e-12))
    k = zk * (1.0 + (a - 1.0) * ka_ref[...])

    cum = _split_dot_rhs(tri, lw)
    e_pos = jnp.exp(cum)
    e_neg = jnp.exp(-cum)
    a_t = -kk * jnp.exp(cum - lw)
    b_t = kk * a * e_neg
    k_t = k * e_neg
    r_t = zr * e_pos

    ri = lax.broadcasted_iota(jnp.int32, (c, c), 0)
    ci = lax.broadcasted_iota(jnp.int32, (c, c), 1)
    strict = ri > ci
    incl = ri >= ci
    eye = (ri == ci).astype(F32)

    heads = range(RWKV_HEADS)
    sl = [slice(h * n, (h + 1) * n) for h in heads]
    ah = [a_t[:, s] for s in sl]
    bh = [b_t[:, s] for s in sl]
    kh = [k_t[:, s] for s in sl]
    rh = [r_t[:, s] for s in sl]
    vh = [zv[:, s] for s in sl]
    s0 = [st_ref[h] for h in heads]
    p = [jnp.where(strict, _dot_nt(ah[h], bh[h]), 0.0) for h in heads]
    l_ak = [jnp.where(strict, _dot_nt(ah[h], kh[h]), 0.0) for h in heads]
    m_rb = [jnp.where(incl, _dot_nt(rh[h], bh[h]), 0.0) for h in heads]
    m_rk = [jnp.where(incl, _dot_nt(rh[h], kh[h]), 0.0) for h in heads]
    rhs = [_dot_nt(ah[h], s0[h]) + _dot(l_ak[h], vh[h]) for h in heads]
    y0 = [_dot_nt(rh[h], s0[h]) + _dot(m_rk[h], vh[h]) for h in heads]
    s1 = [s0[h] + _dot_tn(vh[h], kh[h]) for h in heads]
    tinv = [eye + p[h] for h in heads]
    for _ in range(int(math.log2(c)) - 1):
        p = [_dot(p[h], p[h]) for h in heads]
        tinv = [tinv[h] + _dot(p[h], tinv[h]) for h in heads]
    u = [_dot(tinv[h], rhs[h]) for h in heads]
    ys = [y0[h] + _dot(m_rb[h], u[h]) for h in heads]
    for h in heads:
        st_ref[h] = (s1[h] + _dot_tn(u[h], bh[h])) * e_pos[c - 1:c, sl[h]]
    yv = jnp.concatenate(ys, axis=-1)

    mu = _split_dot(yv, bd) * (1.0 / n)
    dlt = yv - mu
    var = _split_dot(dlt * dlt, bd) * (1.0 / n)
    yn = dlt * lax.rsqrt(var + RWKV_GN_EPS) * lnw_ref[...] + lnb_ref[...]
    bonus = _split_dot(zr * k * rk_ref[...], bd) * zv
    o_ref[0] = (yn + bonus) * g


def _rwkv(rw, w0, w2, a0, a2, g2, k_k, k_a, r_k, ln_w, ln_b):
    b, s, cols = rw.shape
    c = RWKV_CHUNK
    w_ = RWKV_WIDTH
    row = lambda v: v.reshape(1, w_).astype(F32)
    idx = jnp.arange(w_) // HEAD_DIM
    bd = (idx[:, None] == idx[None, :]).astype(BF16)
    tri = (jnp.arange(c)[:, None] >= jnp.arange(c)[None, :]).astype(BF16)
    full = lambda shp: pl.BlockSpec(shp, lambda i, j: (0,) * len(shp))
    return pl.pallas_call(
        _rwkv_kernel,
        grid=(b, s // c),
        in_specs=[pl.BlockSpec((1, c, cols), lambda i, j: (i, j, 0))]
                 + [full((1, w_))] * 7
                 + [full((RWKV_DECAY_RANK, w_)), full((RWKV_ICLR_RANK, w_)), full((RWKV_GATE_RANK, w_)),
                    full((c, c)), full((w_, w_))],
        out_specs=pl.BlockSpec((1, c, w_), lambda i, j: (i, j, 0)),
        out_shape=jax.ShapeDtypeStruct((b, s, w_), F32),
        scratch_shapes=[pltpu.VMEM((RWKV_HEADS, HEAD_DIM, HEAD_DIM), F32)],
        compiler_params=_params("parallel", "arbitrary"),
        name="rwkv7_chunked",
    )(rw, row(w0), row(a0), row(k_k), row(k_a), row(r_k), row(ln_w), row(ln_b),
      w2.astype(BF16), a2.astype(BF16), g2.astype(BF16), tri, bd)


def _cmp_kernel(f_ref, pe_ref, w1_ref, w2_ref, o_ref):
    f = f_ref[0, 0] + pe_ref[0]
    h = _gelu_exact(_dot(f, w1_ref[0]))
    o_ref[0, 0] = _dot(h, w2_ref[0])


def _nsa_compress(flat, pe, w1, w2):
    two, bg, ncp, lin = flat.shape
    hid = w1.shape[-1]
    return pl.pallas_call(
        _cmp_kernel,
        grid=(two, bg),
        in_specs=[pl.BlockSpec((1, 1, ncp, lin), lambda i, j: (i, j, 0, 0)),
                  pl.BlockSpec((1, 1, lin), lambda i, j: (i, 0, 0)),
                  pl.BlockSpec((1, lin, hid), lambda i, j: (i, 0, 0)),
                  pl.BlockSpec((1, hid, HEAD_DIM), lambda i, j: (i, 0, 0))],
        out_specs=pl.BlockSpec((1, 1, ncp, HEAD_DIM), lambda i, j: (i, j, 0, 0)),
        out_shape=jax.ShapeDtypeStruct((two, bg, ncp, HEAD_DIM), F32),
        compiler_params=_params("parallel", "parallel"),
        name="nsa_compress",
    )(flat, pe, w1.astype(BF16), w2.astype(BF16))


def _attn_step(k_tile, vt_tile, qt, bias, m_ref, l_ref, acc_ref):
    s = jnp.dot(k_tile, qt, preferred_element_type=F32) + bias
    m_old = m_ref[0:1, :]
    m_new = jnp.maximum(m_old, jnp.max(s, axis=0, keepdims=True))
    p = jnp.exp(s - m_new)
    alpha = jnp.exp(m_old - m_new)
    l_ref[0:1, :] = alpha * l_ref[0:1, :] + jnp.sum(p, axis=0, keepdims=True)
    acc_ref[...] = alpha * acc_ref[...] + jnp.dot(vt_tile, p.astype(BF16), preferred_element_type=F32)
    m_ref[0:1, :] = m_new


def _key_rows(ref, j):
    return ref[0, 0, pl.ds(pl.multiple_of(j * Q_BLOCK, Q_BLOCK), Q_BLOCK), :]


def _nsa_kernel(qt_ref, gate_ref, kc_ref, vct_ref, ks_ref, vst_ref, kw_ref, vwt_ref, ovlt_ref,
                d0_ref, d1_ref, edge_ref, far_ref, o_ref, sb_ref, m_ref, l_ref, acc_ref,
                *, n_sel, n_cmp_pad):
    qb = Q_BLOCK
    hg = NSA_HG
    i = pl.program_id(2)
    s0 = i * qb
    qt = qt_ref[0, 0, 0]

    crow = lax.broadcasted_iota(jnp.int32, (n_cmp_pad, qb), 0)
    tcol = s0 + lax.broadcasted_iota(jnp.int32, (n_cmp_pad, qb), 1)
    cmask = (NSA_CMP_STRIDE * crow + NSA_CMP_LEN - 1) <= tcol
    st = jnp.dot(kc_ref[0, 0], qt, preferred_element_type=F32)
    vct = vct_ref[0, 0]
    psum = jnp.zeros((n_cmp_pad, qb), F32)
    o_cmp = []
    for h in range(hg):
        s = jnp.where(cmask, st[:, h * qb:(h + 1) * qb], NEG_BIG)
        mx = jnp.max(s, axis=0, keepdims=True)
        e = jnp.where(cmask, jnp.exp(s - mx), 0.0)
        p = e / jnp.maximum(jnp.sum(e, axis=0, keepdims=True), 1e-30)
        o_cmp.append(jnp.dot(vct, p.astype(BF16), preferred_element_type=F32))
        psum = psum + p

    imp = _split_dot_rhs(ovlt_ref[...], psum)
    nrow = lax.broadcasted_iota(jnp.int32, (n_sel, qb), 0)
    cur = lax.shift_right_logical(s0 + lax.broadcasted_iota(jnp.int32, (n_sel, qb), 1),
                                  int(math.log2(NSA_SEL_BLOCK)))
    forced = (nrow == 0) | (nrow == cur) | (nrow == cur - 1)
    score = jnp.where(forced, NSA_FORCE_SCORE, jnp.where(nrow <= cur, imp, -1.0))
    sb = jnp.full((n_sel, qb), NEG_BIG, F32)
    for _ in range(min(NSA_SEL_TOPN, n_sel)):
        mx = jnp.max(score, axis=0, keepdims=True)
        first = jnp.min(jnp.where(score == mx, nrow, n_sel), axis=0, keepdims=True)
        hit = nrow == first
        sb = jnp.where(hit, 0.0, sb)
        score = jnp.where(hit, -jnp.inf, score)
    sb_ref[...] = sb

    blocks_per_tile = qb // NSA_SEL_BLOCK

    def sel_bias(j0, ntiles):
        rows = [jnp.broadcast_to(sb_ref[pl.ds(blocks_per_tile * j0 + r, 1), :], (NSA_SEL_BLOCK, qb))
                for r in range(blocks_per_tile * ntiles)]
        return jnp.concatenate([jnp.concatenate(rows, axis=0)] * hg, axis=1)

    def reset():
        m_ref[...] = jnp.full_like(m_ref, NEG_BIG)
        l_ref[...] = jnp.zeros_like(l_ref)
        acc_ref[...] = jnp.zeros_like(acc_ref)

    def finish():
        return acc_ref[...] / jnp.maximum(l_ref[0:1, :], 1e-30)

    def step(k_ref, vt_ref, j0, ntiles, bias):
        k_rows = k_ref[0, 0, pl.ds(pl.multiple_of(j0 * qb, qb), ntiles * qb), :]
        vt = jnp.concatenate([vt_ref[0, 0, j0 + t] for t in range(ntiles)], axis=1)
        _attn_step(k_rows, vt, qt, bias, m_ref, l_ref, acc_ref)

    def near_step(k_ref, vt_ref, selected):
        @pl.when(i >= 1)
        def _():
            bias = jnp.concatenate([d1_ref[0], d0_ref[0]], axis=0)
            step(k_ref, vt_ref, i - 1, 2, bias + sel_bias(i - 1, 2) if selected else bias)

        @pl.when(i == 0)
        def _():
            bias = d0_ref[0]
            step(k_ref, vt_ref, 0, 1, bias + sel_bias(0, 1) if selected else bias)

    far = far_ref[0, 0:1, :]
    far_group = 8
    half_group = far_group // 2

    reset()
    near_step(ks_ref, vst_ref, True)
    n_far = jnp.maximum(i - 1, 0)
    n_grp = lax.shift_right_logical(n_far, int(math.log2(far_group)))
    done = n_grp * far_group
    take_half = (n_far - done) >= half_group

    def far_group_body(gi, carry):
        step(ks_ref, vst_ref, gi * far_group, far_group, sel_bias(gi * far_group, far_group) + far)
        return carry

    def far_tile_body(j, carry):
        step(ks_ref, vst_ref, j, 1, sel_bias(j, 1) + far)
        return carry

    lax.fori_loop(0, n_grp, far_group_body, 0)

    @pl.when(take_half)
    def _():
        step(ks_ref, vst_ref, done, half_group, sel_bias(done, half_group) + far)

    lax.fori_loop(done + jnp.where(take_half, half_group, 0), n_far, far_tile_body, 0)
    o_sel = finish()

    reset()
    near_step(kw_ref, vwt_ref, False)
    n_back = NSA_WINDOW // qb
    n_mid = n_back - 2

    @pl.when(i >= n_back - 1)
    def _():
        step(kw_ref, vwt_ref, i - (n_back - 1), n_mid, far)

    for d in range(2, n_back):
        @pl.when((i >= d) & (i < n_back - 1))
        def _(d=d):
            step(kw_ref, vwt_ref, i - d, 1, far)

    @pl.when(i >= n_back)
    def _():
        step(kw_ref, vwt_ref, i - n_back, 1, edge_ref[...] + far)
    o_win = finish()

    gs = _sigmoid(gate_ref[0, 0, 0])
    o_ref[0, 0, 0] = (gs[0:1, :] * jnp.concatenate(o_cmp, axis=1) + gs[1:2, :] * o_sel + gs[2:3, :] * o_win)


def _t5_bucket(dist):
    n = jnp.maximum(dist, 0)
    max_exact = REL_BUCKETS // 2
    nf = jnp.maximum(n, max_exact).astype(F32)
    large = max_exact + (jnp.log(nf / max_exact) / math.log(REL_MAX_DIST / max_exact)
                         * (REL_BUCKETS - max_exact)).astype(jnp.int32)
    large = jnp.minimum(large, REL_BUCKETS - 1)
    return jnp.where(n < max_exact, n, large)


def _bias_tiles(tbl, groups, hg):
    qb = Q_BLOCK
    kj = jnp.arange(qb)[:, None]
    qi = jnp.arange(qb)[None, :]
    t = tbl.T.reshape(groups, hg, REL_BUCKETS).astype(F32)
    lay = lambda x: jnp.transpose(x, (0, 2, 1, 3)).reshape(groups, qb, hg * qb)
    buckets = jnp.arange(REL_BUCKETS)

    def lookup(bucket):
        hit = bucket[None, None, :, :, None] == buckets
        return jnp.sum(jnp.where(hit, t[:, :, None, None, :], 0.0), axis=-1)

    d0 = lay(lookup(_t5_bucket(qi - kj)) + jnp.where(kj <= qi, 0.0, NEG_BIG))
    d1 = lay(lookup(_t5_bucket(qi - kj + qb)))
    edge = jnp.tile(jnp.where(qi < kj, 0.0, NEG_BIG).astype(F32), (1, hg))
    far = jnp.broadcast_to(t[:, :, REL_BUCKETS - 1][:, None, :, None], (groups, 8, hg, qb))
    return d0, d1, edge, far.reshape(groups, 8, hg * qb)


def _tiles_t(z, b, s, g):
    nt = s // Q_BLOCK
    z = z.reshape(b, nt, Q_BLOCK, g, HEAD_DIM)
    return jnp.transpose(z, (0, 3, 1, 4, 2)).astype(BF16)


def _rows(z, b, s, g):
    return jnp.transpose(z.reshape(b, s, g, HEAD_DIM), (0, 2, 1, 3)).astype(BF16)


def _q_t(q, b, s, g, hg):
    nq = s // Q_BLOCK
    q = q.reshape(b, nq, Q_BLOCK, g, hg, HEAD_DIM) * (HEAD_DIM ** -0.5)
    return jnp.transpose(q, (0, 3, 1, 5, 4, 2)).reshape(b, g, nq, HEAD_DIM, hg * Q_BLOCK).astype(BF16)


def _o_from_t(o_t, b, s, g, hg):
    nq = s // Q_BLOCK
    o = o_t.reshape(b, g, nq, HEAD_DIM, hg, Q_BLOCK)
    return jnp.transpose(o, (0, 2, 5, 1, 4, 3)).reshape(b, s, g * hg * HEAD_DIM)


def _nsa(q, kc, vc, ks, vs, kw, vw, gates, pe_k, pe_v, ck_w1, ck_w2, cv_w1, cv_w2, tbl):
    b, s, _ = q.shape
    g, hg, dh = NSA_KV_GROUPS, NSA_HG, HEAD_DIM
    qb = Q_BLOCK
    nchunk = s // NSA_CMP_STRIDE
    n_sub = NSA_CMP_LEN // NSA_CMP_STRIDE
    n_cmp = nchunk - n_sub + 1
    n_sel = s // NSA_SEL_BLOCK
    nt = s // qb

    def flat_blocks(z):
        ch = z.reshape(b, nchunk, NSA_CMP_STRIDE, g, dh)
        ch = jnp.pad(ch, ((0, 0), (0, n_sub - 1), (0, 0), (0, 0), (0, 0)))
        blk = jnp.concatenate([ch[:, j:j + nchunk] for j in range(n_sub)], axis=2)
        return jnp.transpose(blk, (0, 3, 1, 2, 4)).reshape(b * g, nchunk, NSA_CMP_LEN * dh)

    flat = jnp.stack([flat_blocks(kc), flat_blocks(vc)])
    pe = jnp.stack([pe_k.reshape(1, -1), pe_v.reshape(1, -1)])
    cmp_kv = _nsa_compress(flat, pe, jnp.stack([ck_w1, cv_w1]), jnp.stack([ck_w2, cv_w2]))
    cmp_kv = cmp_kv.reshape(2, b, g, nchunk, dh)
    kcm = cmp_kv[0].astype(BF16)
    vct = jnp.transpose(cmp_kv[1], (0, 1, 3, 2)).astype(BF16)

    cmp_start = jnp.arange(nchunk) * NSA_CMP_STRIDE
    cmp_end = cmp_start + NSA_CMP_LEN - 1
    sel_start = jnp.arange(n_sel) * NSA_SEL_BLOCK
    ovlt = ((cmp_end[None, :] >= sel_start[:, None])
            & (cmp_start[None, :] <= sel_start[:, None] + NSA_SEL_BLOCK - 1)
            & (jnp.arange(nchunk)[None, :] < n_cmp)).astype(BF16)

    d0, d1, edge, far = tbl
    gt = gates.reshape(b, nt, qb, g, hg, 3)
    gt = jnp.transpose(gt, (0, 3, 1, 5, 4, 2)).reshape(b, g, nt, 3, hg * qb)
    gt = jnp.pad(gt, ((0, 0), (0, 0), (0, 0), (0, 5), (0, 0)))

    wide = hg * qb
    per_q = lambda rows: pl.BlockSpec((1, 1, 1, rows, wide), lambda bi, gi, i: (bi, gi, i, 0, 0))
    rows_spec = pl.BlockSpec((1, 1, s, dh), lambda bi, gi, i: (bi, gi, 0, 0))
    tiles_spec = pl.BlockSpec((1, 1, nt, dh, qb), lambda bi, gi, i: (bi, gi, 0, 0, 0))
    bias_spec = pl.BlockSpec((1, qb, wide), lambda bi, gi, i: (gi, 0, 0))
    o_t = pl.pallas_call(
        functools.partial(_nsa_kernel, n_sel=n_sel, n_cmp_pad=nchunk),
        grid=(b, g, nt),
        in_specs=[per_q(dh), per_q(8),
                  pl.BlockSpec((1, 1, nchunk, dh), lambda bi, gi, i: (bi, gi, 0, 0)),
                  pl.BlockSpec((1, 1, dh, nchunk), lambda bi, gi, i: (bi, gi, 0, 0)),
                  rows_spec, tiles_spec, rows_spec, tiles_spec,
                  pl.BlockSpec((n_sel, nchunk), lambda bi, gi, i: (0, 0)),
                  bias_spec, bias_spec,
                  pl.BlockSpec((qb, wide), lambda bi, gi, i: (0, 0)),
                  pl.BlockSpec((1, 8, wide), lambda bi, gi, i: (gi, 0, 0))],
        out_specs=per_q(dh),
        out_shape=jax.ShapeDtypeStruct((b, g, nt, dh, wide), F32),
        scratch_shapes=[pltpu.VMEM((n_sel, qb), F32), pltpu.VMEM((8, wide), F32),
                        pltpu.VMEM((8, wide), F32), pltpu.VMEM((dh, wide), F32)],
        compiler_params=_params("parallel", "parallel", "arbitrary"),
        name="nsa_attention",
    )(_q_t(q, b, s, g, hg), gt, kcm, vct,
      _rows(ks, b, s, g), _tiles_t(vs, b, s, g), _rows(kw, b, s, g), _tiles_t(vw, b, s, g),
      ovlt, d0, d1, edge, far)
    return _o_from_t(o_t, b, s, g, hg)


def _swa_kernel(qt_ref, k_ref, vt_ref, d0_ref, d1_ref, sink_ref, o_ref, m_ref, l_ref, acc_ref):
    i = pl.program_id(2)
    qt = qt_ref[0, 0, 0]
    m_ref[...] = sink_ref[0]
    l_ref[...] = jnp.ones_like(l_ref)
    acc_ref[...] = jnp.zeros_like(acc_ref)
    _attn_step(_key_rows(k_ref, i), vt_ref[0, 0, i], qt, d0_ref[0], m_ref, l_ref, acc_ref)

    @pl.when(i >= 1)
    def _():
        _attn_step(_key_rows(k_ref, i - 1), vt_ref[0, 0, i - 1], qt, d1_ref[0], m_ref, l_ref, acc_ref)

    o_ref[0, 0, 0] = acc_ref[...] / l_ref[0:1, :]


def _swa(q, k, v, sinks, tbl):
    b, s, _ = q.shape
    g, hg, dh = SWA_KV_HEADS, SWA_HG, HEAD_DIM
    qb = Q_BLOCK
    nt = s // qb
    assert SWA_WINDOW == qb
    d0, d1, edge, _ = tbl
    d1 = d1 + edge[None]
    wide = hg * qb
    sink = jnp.broadcast_to(sinks.reshape(g, 1, hg, 1).astype(F32), (g, 8, hg, qb)).reshape(g, 8, wide)
    per_q = pl.BlockSpec((1, 1, 1, dh, wide), lambda bi, gi, i: (bi, gi, i, 0, 0))
    bias_spec = pl.BlockSpec((1, qb, wide), lambda bi, gi, i: (gi, 0, 0))
    o_t = pl.pallas_call(
        _swa_kernel,
        grid=(b, g, nt),
        in_specs=[per_q,
                  pl.BlockSpec((1, 1, s, dh), lambda bi, gi, i: (bi, gi, 0, 0)),
                  pl.BlockSpec((1, 1, nt, dh, qb), lambda bi, gi, i: (bi, gi, 0, 0, 0)),
                  bias_spec, bias_spec,
                  pl.BlockSpec((1, 8, wide), lambda bi, gi, i: (gi, 0, 0))],
        out_specs=per_q,
        out_shape=jax.ShapeDtypeStruct((b, g, nt, dh, wide), F32),
        scratch_shapes=[pltpu.VMEM((8, wide), F32), pltpu.VMEM((8, wide), F32), pltpu.VMEM((dh, wide), F32)],
        compiler_params=_params("parallel", "parallel", "arbitrary"),
        name="swa_sink_attention",
    )(_q_t(q, b, s, g, hg), _rows(k, b, s, g), _tiles_t(v, b, s, g), d0, d1, sink)
    return _o_from_t(o_t, b, s, g, hg)


def _merge_kernel(x_ref, oa_ref, ob_ref, oc_ref, ga_ref, gb_ref, gc_ref,
                  wa_ref, wb_ref, wc_ref, wo_ref, o_ref):
    merged = (_sigmoid(ga_ref[...]) * _dot(oa_ref[...], wa_ref[...])
              + _sigmoid(gb_ref[...]) * _dot(ob_ref[...], wb_ref[...])
              + _sigmoid(gc_ref[...]) * _dot(oc_ref[...], wc_ref[...]))
    o_ref[...] = x_ref[...] + _dot(merged, wo_ref[...])


def _merge(x, o_a, o_b, o_c, g_a, g_b, g_c, w_a, w_b, w_c, w_o, tm):
    t, d = x.shape
    tok = lambda w: pl.BlockSpec((tm, w), lambda i: (i, 0))
    full = lambda a: pl.BlockSpec(a.shape, lambda i: (0, 0))
    ws = [w.astype(BF16) for w in (w_a, w_b, w_c, w_o)]
    return pl.pallas_call(
        _merge_kernel,
        grid=(t // tm,),
        in_specs=[tok(d), tok(o_a.shape[1]), tok(o_b.shape[1]), tok(o_c.shape[1]), tok(d), tok(d), tok(d)]
                 + [full(w) for w in ws],
        out_specs=tok(d),
        out_shape=jax.ShapeDtypeStruct((t, d), F32),
        compiler_params=_params("parallel"),
        name="branch_merge",
    )(x, o_a, o_b, o_c, g_a, g_b, g_c, *ws)


def _top_rows(x, k):
    r = x.shape[0]
    ridx = lax.broadcasted_iota(jnp.int32, x.shape, 0)
    vals = []
    for _ in range(k):
        mx = jnp.max(x, axis=0, keepdims=True)
        first = jnp.min(jnp.where(x == mx, ridx, r), axis=0, keepdims=True)
        x = jnp.where(ridx == first, -jnp.inf, x)
        vals.append(mx)
    return jnp.concatenate(vals, axis=0)


def _peer_score_kernel(x_ref, g_ref, wqt_ref, sk_ref, z_ref, s1_ref, s2_ref, e1_ref, e2_ref, tau_ref):
    kk = PEER_TOPK
    half = PEER_QDIM // 2
    tt = x_ref.shape[0]
    lane_chunk = PEER_LANE_CHUNK
    zb = _rms(x_ref[...], g_ref[...]).astype(BF16)
    z_ref[...] = zb
    q_t = lax.dot_general(wqt_ref[...], zb, (((1,), (1,)), ((), ())), preferred_element_type=F32)
    q_t = q_t.astype(BF16)
    for h in range(PEER_HEADS):
        s = [jnp.dot(sk_ref[h, p], q_t[(2 * h + p) * half:(2 * h + p + 1) * half, :],
                     preferred_element_type=F32) for p in range(2)]
        top = [_top_rows(s[p], kk) for p in range(2)]
        pieces = [top[0][i:i + 1] + top[1][0:kk // (i + 1)] for i in range(kk)]
        n_cand = sum(pc.shape[0] for pc in pieces)
        pieces.append(jnp.full(((-n_cand) % 8, tt), -jnp.inf, F32))
        best = _top_rows(jnp.concatenate(pieces, axis=0), kk)
        zsum = jnp.sum(jnp.exp(best - best[0:1]), axis=0, keepdims=True)
        e1 = jnp.exp(s[0] - top[0][0:1]) / zsum
        e2 = jnp.exp(s[1] - top[1][0:1])
        tau = jnp.broadcast_to(best[kk - 1:kk], (8, tt))
        for c in range(tt // lane_chunk):
            lanes = slice(c * lane_chunk, (c + 1) * lane_chunk)
            s1_ref[h, c] = s[0][:, lanes]
            s2_ref[h, c] = s[1][:, lanes]
            e1_ref[h, c] = e1[:, lanes]
            e2_ref[h, c] = e2[:, lanes]
            tau_ref[h, c] = tau[:, lanes]


def _peer_scores(x, g, wq, subkeys, tt):
    t, d = x.shape
    hp, nk = PEER_HEADS, PEER_NKEYS
    wqt = jnp.transpose(wq).astype(BF16)
    sk = subkeys.astype(BF16)
    lc = PEER_LANE_CHUNK
    stat = lambda rows: pl.BlockSpec((hp, tt // lc, rows, lc), lambda i: (0, i, 0, 0))
    shp = lambda rows: jax.ShapeDtypeStruct((hp, t // lc, rows, lc), F32)
    return pl.pallas_call(
        _peer_score_kernel,
        grid=(t // tt,),
        in_specs=[pl.BlockSpec((tt, d), lambda i: (i, 0)),
                  pl.BlockSpec((1, d), lambda i: (0, 0)),
                  pl.BlockSpec(wqt.shape, lambda i: (0, 0)),
                  pl.BlockSpec(sk.shape, lambda i: (0, 0, 0, 0))],
        out_specs=[pl.BlockSpec((tt, d), lambda i: (i, 0)), stat(nk), stat(nk), stat(nk), stat(nk), stat(8)],
        out_shape=[jax.ShapeDtypeStruct((t, d), BF16), shp(nk), shp(nk), shp(nk), shp(nk), shp(8)],
        compiler_params=_params("parallel"),
        name="peer_scores",
    )(x, g.reshape(1, d), wqt, sk)


def _peer_main_kernel(x_ref, z_ref, s1_ref, s2_ref, e1_ref, e2_ref, tau_ref, u_ref, v_ref, lnf_ref,
                      o_ref, acc_ref, w_ref, *, rows_per_tile, final_norm):
    e = pl.program_id(1)

    @pl.when(e == 0)
    def _():
        acc_ref[...] = jnp.zeros_like(acc_ref)

    nk = PEER_NKEYS
    n_chunks = s2_ref.shape[1]
    row_shift = int(math.log2(rows_per_tile))
    h_t = _dot_nt(u_ref[...], z_ref[...])

    def weight_block(it, carry):
        c = lax.shift_right_logical(it, row_shift)
        r = jnp.bitwise_and(it, rows_per_tile - 1)
        w = jnp.zeros((nk, PEER_LANE_CHUNK), F32)
        for h in range(PEER_HEADS):
            pair = s2_ref[h, c] + s1_ref[h, c, pl.ds(r, 1), :]
            val = e2_ref[h, c] * e1_ref[h, c, pl.ds(r, 1), :]
            w = w + jnp.where(pair >= tau_ref[h, c, 0:1, :], val, 0.0)
        w_ref[c, pl.ds(pl.multiple_of(r * nk, nk), nk), :] = w
        return carry

    for it in range(rows_per_tile):
        weight_block(jnp.int32(it), 0)
    lax.fori_loop(rows_per_tile, n_chunks * rows_per_tile, weight_block, 0, unroll=2)
    act = jnp.concatenate([w_ref[c] for c in range(n_chunks)], axis=1) * _gelu_exact(h_t)
    acc_ref[...] += _dot_tn(act, v_ref[...])

    @pl.when(e == pl.num_programs(1) - 1)
    def _():
        y = x_ref[...] + acc_ref[...]
        if final_norm:
            y = _rms(y, lnf_ref[...])
        o_ref[...] = y


def _peer_main(x, z, s1, s2, e1, e2, tau, u_tab, v_tab, lnf_w, final_norm, tt, te):
    t, d = x.shape
    hp, nk = PEER_HEADS, PEER_NKEYS
    n_exp = u_tab.shape[0]
    lc = PEER_LANE_CHUNK
    stat = lambda rows: pl.BlockSpec((hp, tt // lc, rows, lc), lambda i, e: (0, i, 0, 0))
    rows_per_tile = te // nk
    assert rows_per_tile == 8
    tile_rows = pl.BlockSpec((hp, tt // lc, rows_per_tile, lc), lambda i, e: (0, i, e, 0))
    return pl.pallas_call(
        functools.partial(_peer_main_kernel, rows_per_tile=rows_per_tile, final_norm=final_norm),
        grid=(t // tt, n_exp // te),
        in_specs=[pl.BlockSpec((tt, d), lambda i, e: (i, 0)),
                  pl.BlockSpec((tt, d), lambda i, e: (i, 0)),
                  tile_rows, stat(nk), tile_rows, stat(nk), stat(8),
                  pl.BlockSpec((te, d), lambda i, e: (e, 0)),
                  pl.BlockSpec((te, d), lambda i, e: (e, 0)),
                  pl.BlockSpec((1, d), lambda i, e: (0, 0))],
        out_specs=pl.BlockSpec((tt, d), lambda i, e: (i, 0)),
        out_shape=jax.ShapeDtypeStruct((t, d), F32),
        scratch_shapes=[pltpu.VMEM((tt, d), F32), pltpu.VMEM((tt // lc, te, lc), F32)],
        compiler_params=_params("parallel", "arbitrary"),
        name="peer_dense",
    )(x, z, s1, s2, e1, e2, tau, u_tab.astype(BF16), v_tab.astype(BF16), lnf_w.reshape(1, d))


def _pad_cols(w, mult):
    pad = (-w.shape[1]) % mult
    return jnp.pad(w, ((0, 0), (0, pad))) if pad else w


def _split(z, sizes):
    out, o = [], 0
    for sz in sizes:
        out.append(z[..., o:o + sz])
        o += sz
    return out


def _layer(x, b, s, p, l, nsa_tbl, swa_tbl, lnf_w, final_norm):
    t, d = x.shape
    hd = HEAD_DIM
    proj = _rms_matmul(x, p["ln1_w"][l], _pad_cols(p["w_in"][l], 512).astype(BF16), 512, 512)
    nsa_sizes = (NSA_Q_HEADS * hd,) + (NSA_KV_GROUPS * hd,) * 6 + (NSA_Q_HEADS * 3,)
    swa_sizes = (SWA_Q_HEADS * hd, SWA_KV_HEADS * hd, SWA_KV_HEADS * hd)
    cols = _split(proj, (RWKV_COLS,) + nsa_sizes + swa_sizes + (d, d, d))
    rw = cols[0].reshape(b, s, RWKV_COLS)
    rw_prev = jnp.pad(rw, ((0, 0), (1, 0), (0, 0)))[:, :-1]
    rw = rw + p["rwkv_mu"][l] * (rw_prev - rw)
    o_a = _rwkv(rw, p["rwkv_w0"][l], p["rwkv_w2"][l], p["rwkv_a0"][l], p["rwkv_a2"][l], p["rwkv_g2"][l],
                p["rwkv_k_k"][l], p["rwkv_k_a"][l], p["rwkv_r_k"][l], p["rwkv_ln_w"][l], p["rwkv_ln_b"][l])
    seq = lambda z: z.reshape(b, s, z.shape[-1])
    nq, nkc, nvc, nks, nvs, nkw, nvw, ngate = [seq(c) for c in cols[1:9]]
    o_b = _nsa(nq, nkc, nvc, nks, nvs, nkw, nvw, ngate, p["nsa_pe_k"][l], p["nsa_pe_v"][l],
               p["nsa_ck_w1"][l], p["nsa_ck_w2"][l], p["nsa_cv_w1"][l], p["nsa_cv_w2"][l], nsa_tbl)
    sq, sk, sv = [seq(c) for c in cols[9:12]]
    o_c = _swa(sq, sk, sv, p["swa_sinks"][l], swa_tbl)
    g_a, g_b, g_c = cols[12:15]
    x = _merge(x, o_a.reshape(t, -1), o_b.reshape(t, -1), o_c.reshape(t, -1), g_a, g_b, g_c,
               p["w_br_a"][l], p["w_br_b"][l], p["w_br_c"][l], p["w_out"][l], 256)
    z, s1, s2, e1, e2, tau = _peer_scores(x, p["ln2_w"][l], p["peer_wq"][l], p["peer_subkeys"][l], 256)
    return _peer_main(x, z, s1, s2, e1, e2, tau, p["peer_u"][l], p["peer_v"][l], lnf_w, final_norm, 512, 1024)


def kernel(x, ln1_w, ln2_w, lnf_w, rel_bias, w_in, rwkv_mu, rwkv_w0, rwkv_w2, rwkv_a0, rwkv_a2, rwkv_g2, rwkv_k_k, rwkv_k_a, rwkv_r_k, rwkv_ln_w, rwkv_ln_b, nsa_pe_k, nsa_pe_v, nsa_ck_w1, nsa_ck_w2, nsa_cv_w1, nsa_cv_w2, swa_sinks, w_br_a, w_br_b, w_br_c, w_out, peer_wq, peer_subkeys, peer_u, peer_v):
    p = dict(ln1_w=ln1_w, ln2_w=ln2_w, w_in=w_in, rwkv_mu=rwkv_mu, rwkv_w0=rwkv_w0, rwkv_w2=rwkv_w2,
             rwkv_a0=rwkv_a0, rwkv_a2=rwkv_a2, rwkv_g2=rwkv_g2, rwkv_k_k=rwkv_k_k, rwkv_k_a=rwkv_k_a,
             rwkv_r_k=rwkv_r_k.reshape(rwkv_r_k.shape[0], -1), rwkv_ln_w=rwkv_ln_w, rwkv_ln_b=rwkv_ln_b,
             nsa_pe_k=nsa_pe_k, nsa_pe_v=nsa_pe_v, nsa_ck_w1=nsa_ck_w1, nsa_ck_w2=nsa_ck_w2,
             nsa_cv_w1=nsa_cv_w1, nsa_cv_w2=nsa_cv_w2, swa_sinks=swa_sinks, w_br_a=w_br_a, w_br_b=w_br_b,
             w_br_c=w_br_c, w_out=w_out, peer_wq=peer_wq, peer_subkeys=peer_subkeys, peer_u=peer_u,
             peer_v=peer_v)
    b, s, d = x.shape
    depth = w_in.shape[0]
    nsa_tbl = _bias_tiles(rel_bias[:, :NSA_Q_HEADS], NSA_KV_GROUPS, NSA_HG)
    swa_tbl = _bias_tiles(rel_bias[:, NSA_Q_HEADS:], SWA_KV_HEADS, SWA_HG)
    y = x.reshape(b * s, d)
    for l in range(depth):
        y = _layer(y, b, s, p, l, nsa_tbl, swa_tbl, lnf_w, l == depth - 1)
    return y.reshape(b, s, d)
```
